```python
import math
import jax, jax.numpy as jnp
from jax import lax
import numpy as np

D_MODEL = 1024
BATCH = 2
SEQ = 8192
DEPTH = 4

CTX_LEN = 256
GRID_W = 64
EPS = 1e-6
ROPE_BASE = 10000.0
ROPE_DIM = 32
Q_BLOCK = 128
CHUNK = 64

MLA_HEADS = 4
MLA_Q_RANK = 256
MLA_KV_RANK = 128
MLA_NOPE = 64
MLA_ROPE = ROPE_DIM
MLA_V = 64
MLA_SCALE = (MLA_NOPE + MLA_ROPE) ** -0.5

DIFF_HEADS = 4
DIFF_DK = ROPE_DIM
DIFF_DV = 2 * DIFF_DK
DIFF_SCALE = DIFF_DK ** -0.5

HGRN_HEADS = 8
HGRN_DK = 64
HGRN_DV = 64

MLA_WIDTH = MLA_HEADS * MLA_V
DIFF_WIDTH = DIFF_HEADS * DIFF_DV
HGRN_WIDTH = HGRN_HEADS * HGRN_DV
MIX_WIDTH = MLA_WIDTH + DIFF_WIDTH + HGRN_WIDTH

D_FF = (((8 * D_MODEL + 2) // 3 + 255) // 256) * 256

IN_SIZES = (
    MLA_Q_RANK, MLA_KV_RANK, MLA_ROPE,
    DIFF_HEADS * 2 * DIFF_DK, DIFF_HEADS * 2 * DIFF_DK, DIFF_WIDTH,
    HGRN_HEADS * HGRN_DK, HGRN_HEADS * HGRN_DK, HGRN_HEADS * HGRN_DK,
    HGRN_WIDTH, HGRN_WIDTH,
)
IN_WIDTH = sum(IN_SIZES)
IN_OFFSETS = tuple(int(o) for o in np.cumsum(IN_SIZES)[:-1])

kernel_name = 'hybrid_mla_diffattn_hgrn2_flow_block'


def rms_norm(x, g):
    xf = x.astype(jnp.float32)
    y = xf * lax.rsqrt(jnp.mean(xf * xf, axis=-1, keepdims=True) + EPS)
    return (y * g.astype(jnp.float32)).astype(x.dtype)


def modulate(x, shift, scale):
    return x * (1 + scale) + shift


def heads(a, n_heads):
    b, t, _ = a.shape
    return a.reshape(b, t, n_heads, -1).transpose(0, 2, 1, 3)


def merge_heads(a):
    b, h, t, d = a.shape
    return a.transpose(0, 2, 1, 3).reshape(b, t, h * d)


def axial_rope_tables(rows):
    t = jnp.arange(rows * GRID_W)
    row = (t // GRID_W).astype(jnp.float32)
    col = (t % GRID_W).astype(jnp.float32)
    n_freq = ROPE_DIM // 4
    freqs = ROPE_BASE ** (-jnp.arange(n_freq, dtype=jnp.float32) / n_freq)
    ang_r = row[:, None] * freqs
    ang_c = col[:, None] * freqs
    return (jnp.cos(ang_r), jnp.sin(ang_r), jnp.cos(ang_c), jnp.sin(ang_c))


def rope_1d(x, cos, sin):
    x1, x2 = jnp.split(x, 2, axis=-1)
    cos = cos.astype(x.dtype)
    sin = sin.astype(x.dtype)
    return jnp.concatenate([x1 * cos - x2 * sin, x2 * cos + x1 * sin], axis=-1)


def rope_2d(x, rope):
    cos_r, sin_r, cos_c, sin_c = rope
    x_row, x_col = jnp.split(x, 2, axis=-1)
    return jnp.concatenate([rope_1d(x_row, cos_r, sin_r), rope_1d(x_col, cos_c, sin_c)], axis=-1)


def layer_lower_bounds(raw):
    p = jax.nn.softmax(raw.astype(jnp.float32), axis=0)
    cum = jnp.cumsum(p, axis=0)
    return cum - cum[0:1]


def log_forget(z, lb):
    z = z.astype(jnp.float32)
    return jnp.logaddexp(jnp.log(lb), jnp.log1p(-lb) + jax.nn.log_sigmoid(z))


def sweep_query_blocks(fn, *qs):
    b, h, t = qs[0].shape[:3]
    nb = t // Q_BLOCK
    blocks = tuple(jnp.moveaxis(a.reshape(b, h, nb, Q_BLOCK, a.shape[-1]), 2, 0) for a in qs)
    o = lax.map(lambda xs: fn(*xs), blocks)
    return jnp.moveaxis(o, 0, 2).reshape(b, h, t, o.shape[-1])


def softmax_attention(q, k, v, scale):
    def one(qb):
        s = jnp.einsum('bhqd,bhkd->bhqk', qb, k).astype(jnp.float32) * scale
        p = jax.nn.softmax(s, axis=-1).astype(v.dtype)
        return jnp.einsum('bhqk,bhkd->bhqd', p, v)
    return sweep_query_blocks(one, q)


def differential_attention(q1, q2, k1, k2, v, lam):
    def one(qb1, qb2):
        s1 = jnp.einsum('bhqd,bhkd->bhqk', qb1, k1).astype(jnp.float32) * DIFF_SCALE
        s2 = jnp.einsum('bhqd,bhkd->bhqk', qb2, k2).astype(jnp.float32) * DIFF_SCALE
        p = jax.nn.softmax(s1, axis=-1) - lam * jax.nn.softmax(s2, axis=-1)
        return jnp.einsum('bhqk,bhkd->bhqd', p.astype(v.dtype), v)
    return sweep_query_blocks(one, q1, q2)


def diff_head_out(o, g, lam_init):
    return merge_heads(rms_norm(o, g) * (1 - lam_init))


def gla_chunk_scan(q, k, v, log_f, s0):
    b, h, t, dk = q.shape
    n = t // CHUNK

    def chunks(a):
        return jnp.moveaxis(a.reshape(b, h, n, CHUNK, a.shape[-1]), 2, 0)

    incl = jnp.tril(jnp.ones((CHUNK, CHUNK), dtype=bool))[:, :, None]

    def step(state, xs):
        qc, kc, vc, gc = xs
        cum = jnp.cumsum(gc, axis=2)
        o_inter = jnp.einsum('bhtk,bhkv->bhtv', qc * jnp.exp(cum), state)
        rel = cum[:, :, :, None, :] - cum[:, :, None, :, :]
        decay = jnp.exp(jnp.where(incl, rel, -jnp.inf))
        scores = jnp.einsum('bhtk,bhsk,bhtsk->bhts', qc, kc, decay)
        o_intra = jnp.einsum('bhts,bhsv->bhtv', scores, vc)
        last = cum[:, :, -1, :]
        k_to_end = kc * jnp.exp(last[:, :, None, :] - cum)
        new_state = jnp.exp(last)[..., None] * state + jnp.einsum('bhsk,bhsv->bhkv', k_to_end, vc)
        return new_state, o_inter + o_intra

    final, o = lax.scan(step, s0, (chunks(q), chunks(k), chunks(v), chunks(log_f)))
    o = jnp.moveaxis(o, 0, 2).reshape(b, h, t, v.shape[-1])
    return o, final


def hgrn_bidirectional(f, s0_fwd, s0_bwd):
    q, v = f['hgrn_q'], f['hgrn_v']
    lf_f, lf_b = f['hgrn_lf_fwd'], f['hgrn_lf_bwd']
    o_f, s_f = gla_chunk_scan(q, -jnp.expm1(lf_f), v, lf_f, s0_fwd)
    flip = lambda a: jnp.flip(a, axis=2)
    o_b, s_b = gla_chunk_scan(flip(q), flip(-jnp.expm1(lf_b)), flip(v), flip(lf_b), s0_bwd)
    return o_f + flip(o_b), s_f, s_b


def hgrn_head_out(o, g, gain):
    return merge_heads(rms_norm(o, gain)).astype(g.dtype) * jax.nn.silu(g)


def stream_features(h, lw, rope):
    proj = h @ lw['w_in']
    cq, ckv, kr, dq, dk, dv, hq, hf_fwd, hf_bwd, hi, hg = jnp.split(proj, IN_OFFSETS, axis=-1)
    b, t, _ = h.shape
    q = heads(rms_norm(cq, lw['g_q_norm']) @ lw['w_uq'], MLA_HEADS)
    kv = heads(rms_norm(ckv, lw['g_kv_norm']) @ lw['w_ukv'], MLA_HEADS)
    q_nope, q_rope = jnp.split(q, [MLA_NOPE], axis=-1)
    k_nope, mla_v = jnp.split(kv, [MLA_NOPE], axis=-1)
    k_rope = kr[:, None]
    dq = dq.reshape(b, t, DIFF_HEADS, 2, DIFF_DK).transpose(0, 2, 3, 1, 4)
    dk = dk.reshape(b, t, DIFF_HEADS, 2, DIFF_DK).transpose(0, 2, 3, 1, 4)
    if rope is not None:
        q_rope = rope_2d(q_rope, rope)
        k_rope = rope_2d(k_rope, rope)
        dq = rope_2d(dq, rope)
        dk = rope_2d(dk, rope)
    mla_q = jnp.concatenate([q_nope, q_rope], axis=-1)
    mla_k = jnp.concatenate([k_nope, jnp.broadcast_to(k_rope, (b, MLA_HEADS, t, MLA_ROPE))], axis=-1)
    lf_fwd = heads(log_forget(hf_fwd, lw['lb'][0]), HGRN_HEADS)
    lf_bwd = heads(log_forget(hf_bwd, lw['lb'][1]), HGRN_HEADS)
    return {
        'mla_q': mla_q, 'mla_k': mla_k, 'mla_v': mla_v,
        'diff_q1': dq[:, :, 0], 'diff_q2': dq[:, :, 1],
        'diff_k1': dk[:, :, 0], 'diff_k2': dk[:, :, 1],
        'diff_v': heads(dv, DIFF_HEADS),
        'hgrn_q': heads(jax.nn.silu(hq), HGRN_HEADS).astype(jnp.float32),
        'hgrn_v': heads(hi, HGRN_HEADS).astype(jnp.float32),
        'hgrn_lf_fwd': lf_fwd, 'hgrn_lf_bwd': lf_bwd,
        'hgrn_g': hg,
    }


def token_mixers(fl, fc, lw, lam, lam_init, with_ctx_out):
    cat = lambda a, b_: jnp.concatenate([a, b_], axis=2)
    mla_l = merge_heads(softmax_attention(fl['mla_q'], cat(fc['mla_k'], fl['mla_k']),
                                          cat(fc['mla_v'], fl['mla_v']), MLA_SCALE))
    diff_l = diff_head_out(differential_attention(
        fl['diff_q1'], fl['diff_q2'], cat(fc['diff_k1'], fl['diff_k1']), cat(fc['diff_k2'], fl['diff_k2']),
        cat(fc['diff_v'], fl['diff_v']), lam), lw['g_diff_norm'], lam_init)
    b = fc['hgrn_q'].shape[0]
    zeros = jnp.zeros((b, HGRN_HEADS, HGRN_DK, HGRN_DV), jnp.float32)
    o_c, s_cf, s_cb = hgrn_bidirectional(fc, zeros, zeros)
    o_l, _, _ = hgrn_bidirectional(fl, s_cf, s_cb)
    hgrn_l = hgrn_head_out(o_l, fl['hgrn_g'], lw['g_hgrn_norm'])
    mix_l = jnp.concatenate([mla_l, diff_l, hgrn_l], axis=-1)
    if not with_ctx_out:
        return mix_l, None
    mla_c = merge_heads(softmax_attention(fc['mla_q'], fc['mla_k'], fc['mla_v'], MLA_SCALE))
    diff_c = diff_head_out(differential_attention(
        fc['diff_q1'], fc['diff_q2'], fc['diff_k1'], fc['diff_k2'], fc['diff_v'], lam),
        lw['g_diff_norm'], lam_init)
    hgrn_c = hgrn_head_out(o_c, fc['hgrn_g'], lw['g_hgrn_norm'])
    mix_c = jnp.concatenate([mla_c, diff_c, hgrn_c], axis=-1)
    return mix_l, mix_c


def swiglu(h, w_gate, w_up, w_down):
    return (jax.nn.silu(h @ w_gate) * (h @ w_up)) @ w_down


def setup_inputs(seed: int = 0) -> dict:
    key = jax.random.key(seed)
    ks = jax.random.split(key, 24)
    f32 = jnp.float32

    def nrm(k, shape, scale):
        return jax.random.normal(k, shape, f32) * scale

    def gain(k, shape):
        return 1.0 + 0.05 * jax.random.normal(k, shape, f32)

    return {
        'x': nrm(ks[0], (BATCH, SEQ, D_MODEL), 1.0),
        'c': nrm(ks[1], (BATCH, D_MODEL), 1.0),
        'ctx': nrm(ks[2], (BATCH, CTX_LEN, D_MODEL), 1.0),
        'c_ctx': nrm(ks[3], (D_MODEL,), 1.0),
        'w_ada': nrm(ks[4], (DEPTH, D_MODEL, 6 * D_MODEL), 0.5 * D_MODEL ** -0.5),
        'b_ada': nrm(ks[5], (DEPTH, 6 * D_MODEL), 0.02),
        'g_norm1': gain(ks[6], (DEPTH, D_MODEL)),
        'g_norm2': gain(ks[7], (DEPTH, D_MODEL)),
        'w_in': nrm(ks[8], (DEPTH, D_MODEL, IN_WIDTH), D_MODEL ** -0.5),
        'g_q_norm': gain(ks[9], (DEPTH, MLA_Q_RANK)),
        'w_uq': nrm(ks[10], (DEPTH, MLA_Q_RANK, MLA_HEADS * (MLA_NOPE + MLA_ROPE)), MLA_Q_RANK ** -0.5),
        'g_kv_norm': gain(ks[11], (DEPTH, MLA_KV_RANK)),
        'w_ukv': nrm(ks[12], (DEPTH, MLA_KV_RANK, MLA_HEADS * (MLA_NOPE + MLA_V)), MLA_KV_RANK ** -0.5),
        'diff_lambda': nrm(ks[13], (DEPTH, 4, DIFF_DK), 0.1),
        'g_diff_norm': gain(ks[14], (DEPTH, DIFF_DV)),
        'hgrn_lower_bounds': nrm(ks[15], (DEPTH, 2, HGRN_HEADS * HGRN_DK), 0.1),
        'g_hgrn_norm': gain(ks[16], (DEPTH, HGRN_DV)),
        'w_out': nrm(ks[17], (DEPTH, MIX_WIDTH, D_MODEL), MIX_WIDTH ** -0.5),
        'w_ffn_gate': nrm(ks[18], (DEPTH, D_MODEL, D_FF), D_MODEL ** -0.5),
        'w_ffn_up': nrm(ks[19], (DEPTH, D_MODEL, D_FF), D_MODEL ** -0.5),
        'w_ffn_down': nrm(ks[20], (DEPTH, D_FF, D_MODEL), D_FF ** -0.5),
        'g_final': gain(ks[21], (D_MODEL,)),
    }


def reference(x, c, ctx, c_ctx, w_ada, b_ada, g_norm1, g_norm2, w_in, g_q_norm, w_uq, g_kv_norm, w_ukv,
              diff_lambda, g_diff_norm, hgrn_lower_bounds, g_hgrn_norm, w_out, w_ffn_gate, w_ffn_up,
              w_ffn_down, g_final):
    ROWS = x.shape[1] // GRID_W
    rope = axial_rope_tables(ROWS)
    lb_all = layer_lower_bounds(hgrn_lower_bounds)
    silu_c = jax.nn.silu(c)
    silu_cc = jax.nn.silu(c_ctx)
    for l in range(DEPTH):
        with_ctx_out = l < DEPTH - 1
        mod_l = silu_c @ w_ada[l] + b_ada[l]
        mod_c = silu_cc @ w_ada[l] + b_ada[l]
        sh1, sc1, gt1, sh2, sc2, gt2 = jnp.split(mod_l[:, None, :], 6, axis=-1)
        csh1, csc1, cgt1, csh2, csc2, cgt2 = jnp.split(mod_c, 6, axis=-1)
        lw = {
            'w_in': w_in[l], 'g_q_norm': g_q_norm[l], 'w_uq': w_uq[l], 'g_kv_norm': g_kv_norm[l],
            'w_ukv': w_ukv[l], 'lb': lb_all[l], 'g_diff_norm': g_diff_norm[l], 'g_hgrn_norm': g_hgrn_norm[l],
        }
        lq1, lk1, lq2, lk2 = diff_lambda[l].astype(jnp.float32)
        lam_init = 0.8 - 0.6 * math.exp(-0.3 * l)
        lam = jnp.exp(jnp.sum(lq1 * lk1)) - jnp.exp(jnp.sum(lq2 * lk2)) + lam_init

        h = modulate(rms_norm(x, g_norm1[l]), sh1, sc1)
        hc = modulate(rms_norm(ctx, g_norm1[l]), csh1, csc1)
        fl = stream_features(h, lw, rope)
        fc = stream_features(hc, lw, None)
        mix_l, mix_c = token_mixers(fl, fc, lw, lam, lam_init, with_ctx_out)
        x = x + gt1 * (mix_l @ w_out[l])
        h2 = modulate(rms_norm(x, g_norm2[l]), sh2, sc2)
        x = x + gt2 * swiglu(h2, w_ffn_gate[l], w_ffn_up[l], w_ffn_down[l])
        if with_ctx_out:
            ctx = ctx + cgt1 * (mix_c @ w_out[l])
            h2c = modulate(rms_norm(ctx, g_norm2[l]), csh2, csc2)
            ctx = ctx + cgt2 * swiglu(h2c, w_ffn_gate[l], w_ffn_up[l], w_ffn_down[l])
    return rms_norm(x, g_final)
```

```python
import functools
import math

import jax
import jax.numpy as jnp
import numpy as np
from jax import lax
from jax.experimental import pallas as pl
from jax.experimental.pallas import tpu as pltpu

F32 = jnp.float32
BF16 = jnp.bfloat16

GRID_W = 64
EPS = 1e-6
ROPE_BASE = 10000.0
ROPE_DIM = 32

MLA_HEADS = 4
MLA_Q_RANK = 256
MLA_KV_RANK = 128
MLA_NOPE = 64
MLA_ROPE = ROPE_DIM
MLA_V = 64
MLA_SCALE = (MLA_NOPE + MLA_ROPE) ** -0.5

DIFF_HEADS = 4
DIFF_DK = ROPE_DIM
DIFF_DV = 2 * DIFF_DK
DIFF_SCALE = DIFF_DK ** -0.5

HGRN_HEADS = 8
HGRN_DK = 64
HGRN_DV = 64
HGRN_WIDTH = HGRN_HEADS * HGRN_DV

LANES = 128
MXU_DIM = 256
VMEM_LIMIT_BYTES = 56 * 1024 * 1024

TOKEN_TILE = MXU_DIM
SCAN_CHUNK = 64
SCAN_FINAL_BLOCK = 16

_IN_GROUPS = (
    ('cq', MLA_Q_RANK), ('ckv', MLA_KV_RANK), ('kr', LANES), ('kr_sw', LANES),
    ('dq', 256), ('dq_sw', 256), ('dk', 256), ('dk_sw', 256), ('dv', 256),
    ('hq', HGRN_WIDTH), ('hf', 2 * HGRN_WIDTH), ('hi', HGRN_WIDTH), ('hg', HGRN_WIDTH),
)
_IN_OFF = {}
_o = 0
for _n, _w in _IN_GROUPS:
    _IN_OFF[_n] = (_o, _o + _w)
    _o += _w
IN_WIDTH_PADDED = _o

_ROPE_SWAP = np.concatenate([np.arange(8, 16), np.arange(0, 8), np.arange(24, 32), np.arange(16, 24)])


def _silu(x):
    return x * jax.nn.sigmoid(x)


def _rms(x, g):
    return x * lax.rsqrt(jnp.mean(x * x, axis=-1, keepdims=True) + EPS) * g


def _dot(a, b):
    return jnp.dot(a, b, preferred_element_type=F32)


def _dot_nt(a, b):
    return lax.dot_general(a, b, (((1,), (1,)), ((), ())), preferred_element_type=F32)


def _params(n_grid):
    return pltpu.CompilerParams(dimension_semantics=('arbitrary',) * n_grid,
                                vmem_limit_bytes=VMEM_LIMIT_BYTES)


def _const_spec(shape):
    nd = len(shape)
    return pl.BlockSpec(shape, lambda *_: (0,) * nd, pipeline_mode=pl.Buffered(1))


def _adaln_kernel(cv_ref, w_ref, b_ref, o_ref):
    s = _silu(cv_ref[...])
    o_ref[0] = jnp.dot(s, w_ref[0], preferred_element_type=F32,
                       precision=lax.Precision.HIGHEST) + b_ref[0]


def _adaln(cvec, w_ada, b_ada):
    depth, d, n = w_ada.shape
    tn = 1536
    return pl.pallas_call(
        _adaln_kernel,
        grid=(depth, n // tn),
        in_specs=[pl.BlockSpec((8, d), lambda l, j: (0, 0)),
                  pl.BlockSpec((1, d, tn), lambda l, j: (l, 0, j)),
                  pl.BlockSpec((1, 1, tn), lambda l, j: (l, 0, j))],
        out_specs=pl.BlockSpec((1, 8, tn), lambda l, j: (l, 0, j)),
        out_shape=jax.ShapeDtypeStruct((depth, 8, n), F32),
        compiler_params=_params(2),
        name='adaln',
    )(cvec, w_ada, b_ada.reshape(depth, 1, n))


def _pre_kernel(x_ref, mod_ref, g1_ref, win_ref, gq_ref, wuq_ref, gkv_ref, wukv_ref, llb_ref, l1mlb_ref,
                cd_ref, sd_ref, cq_ref, sq_ref,
                qm_ref, km_ref, vtm_ref, dq_ref, dk_ref, vtd_ref, hq_ref, hv_ref, lf_ref, hg_ref):
    d = x_ref.shape[-1]
    x = x_ref[0]
    mod = mod_ref[0]
    sh1, sc1 = mod[:, 0:d], mod[:, d:2 * d]
    hb = (_rms(x, g1_ref[...]) * (1 + sc1) + sh1).astype(BF16)

    def proj(name):
        lo, hi = _IN_OFF[name]
        return _dot(hb, win_ref[:, lo:hi])

    cq_t, sq_t = cq_ref[...], sq_ref[...]
    cqn = _rms(proj('cq'), gq_ref[...]).astype(BF16)
    uq = _dot(cqn, wuq_ref[...])
    for h in range(MLA_HEADS):
        a = uq[:, h * LANES:(h + 1) * LANES]
        b = uq[:, (MLA_HEADS + h) * LANES:(MLA_HEADS + h + 1) * LANES]
        qm_ref[0, h] = (a * cq_t + b * sq_t).astype(BF16)
    ckvn = _rms(proj('ckv'), gkv_ref[...]).astype(BF16)
    ukv = _dot(ckvn, wukv_ref[...])
    krt = proj('kr') * cq_t + proj('kr_sw') * sq_t
    for h in range(MLA_HEADS):
        km_ref[0, h] = (ukv[:, h * LANES:(h + 1) * LANES] + krt).astype(BF16)
    for p in range(MLA_HEADS // 2):
        lo = (MLA_HEADS + p) * LANES
        vtm_ref[0, p, 0] = ukv[:, lo:lo + LANES].T.astype(BF16)
    cd2 = jnp.concatenate([cd_ref[...]] * 2, axis=1)
    sd2 = jnp.concatenate([sd_ref[...]] * 2, axis=1)
    dq_ref[0] = (proj('dq') * cd2 + proj('dq_sw') * sd2).astype(BF16)
    dk_ref[0] = (proj('dk') * cd2 + proj('dk_sw') * sd2).astype(BF16)
    dv = proj('dv')
    for p in range(DIFF_HEADS // 2):
        vtd_ref[0, p, 0] = dv[:, p * LANES:(p + 1) * LANES].T.astype(BF16)
    hq_ref[0] = _silu(proj('hq'))
    z = proj('hf')
    ls = jnp.minimum(z, 0.0) - jnp.log1p(jnp.exp(-jnp.abs(z)))
    a = llb_ref[...]
    b = l1mlb_ref[...] + ls
    lf_ref[0] = jnp.maximum(a, b) + jnp.log1p(jnp.exp(-jnp.abs(a - b)))
    hv_ref[0] = proj('hi').astype(BF16)
    hg_ref[0] = proj('hg')


def _mod_spec(mods3, ctx_tiles):
    ctx_row = mods3.shape[0] - 1
    return pl.BlockSpec((1, 1, mods3.shape[-1]), lambda b, i: (jnp.where(i < ctx_tiles, ctx_row, b), 0, 0))


def _pre_mixer(xall, mods3, ctx_tiles, g1, win, gq, wuq, gkv, wukv, llb, l1mlb, tables):
    bsz, s, d = xall.shape
    tm = TOKEN_TILE
    ns = s // tm
    grid = (bsz, ns)
    tok = lambda w: pl.BlockSpec((1, tm, w), lambda b, i: (b, i, 0))
    tab = pl.BlockSpec((tm, LANES), lambda b, i: (i, 0))
    head4 = pl.BlockSpec((1, MLA_HEADS, tm, LANES), lambda b, i: (b, 0, i, 0))
    vt = pl.BlockSpec((1, 2, 1, LANES, tm), lambda b, i: (b, 0, i, 0, 0))
    in_specs = [
        tok(d),
        _mod_spec(mods3, ctx_tiles),
        _const_spec(g1.shape), _const_spec(win.shape), _const_spec(gq.shape), _const_spec(wuq.shape),
        _const_spec(gkv.shape), _const_spec(wukv.shape), _const_spec(llb.shape), _const_spec(l1mlb.shape),
        tab, tab, tab, tab,
    ]
    sds = jax.ShapeDtypeStruct
    out_shape = (
        sds((bsz, MLA_HEADS, s, LANES), BF16), sds((bsz, MLA_HEADS, s, LANES), BF16),
        sds((bsz, 2, ns, LANES, tm), BF16),
        sds((bsz, s, 256), BF16), sds((bsz, s, 256), BF16), sds((bsz, 2, ns, LANES, tm), BF16),
        sds((bsz, s, HGRN_WIDTH), F32), sds((bsz, s, HGRN_WIDTH), BF16),
        sds((bsz, s, 2 * HGRN_WIDTH), F32), sds((bsz, s, HGRN_WIDTH), F32),
    )
    out_specs = (head4, head4, vt, tok(256), tok(256), vt,
                 tok(HGRN_WIDTH), tok(HGRN_WIDTH), tok(2 * HGRN_WIDTH), tok(HGRN_WIDTH))
    return pl.pallas_call(
        _pre_kernel, grid=grid, in_specs=in_specs, out_specs=out_specs, out_shape=out_shape,
        compiler_params=_params(2), name='pre_mixer',
    )(xall, mods3, g1, win, gq, wuq, gkv, wukv, llb, l1mlb, *tables)


def _attend(streams, n_chunks, tq):
    def body(j, carry):
        new = []
        for (q, key_chunk, value_t_chunk), (m, l, acc) in zip(streams, carry):
            s = _dot_nt(key_chunk(j), q)
            m_new = jnp.maximum(m, jnp.max(s, axis=0, keepdims=True))
            alpha = jnp.exp(m - m_new)
            p = jnp.exp(s - m_new)
            l = alpha * l + jnp.sum(p, axis=0, keepdims=True)
            acc = alpha * acc + _dot(value_t_chunk(j), p.astype(BF16))
            new.append((m_new, l, acc))
        return tuple(new)

    init = tuple((jnp.full((1, tq), -jnp.inf, F32), jnp.zeros((1, tq), F32), jnp.zeros((LANES, tq), F32))
                 for _ in streams)
    out = lax.fori_loop(0, n_chunks, body, init)
    return [(acc, l) for (_, l, acc) in out]


def _key_chunks(i, ctx_chunks, all_chunks):
    return jnp.where(i < ctx_chunks, ctx_chunks, all_chunks)


def _mla_kernel(ctx_chunks, q_ref, k_ref, vt_ref, o_ref):
    tq = q_ref.shape[2]
    tk = vt_ref.shape[-1]
    n_chunks = _key_chunks(pl.program_id(2), ctx_chunks, vt_ref.shape[2])
    streams = []
    for hh in range(2):
        streams.append((q_ref[0, hh],
                        lambda j, hh=hh: k_ref[0, hh, pl.ds(pl.multiple_of(j * tk, tk), tk), :],
                        lambda j: vt_ref[0, 0, j]))
    (acc0, l0), (acc1, l1) = _attend(streams, n_chunks, tq)
    row = lax.broadcasted_iota(jnp.int32, (LANES, tq), 0)
    ot = jnp.where(row < MLA_V, acc0 / l0, acc1 / l1)
    o_ref[0] = ot.T.astype(o_ref.dtype)


def _mla_attention(qm, km, vtm, ctx_chunks):
    bsz, _, s, _ = qm.shape
    nc, tk = vtm.shape[2], vtm.shape[4]
    tq = TOKEN_TILE
    return pl.pallas_call(
        functools.partial(_mla_kernel, ctx_chunks),
        grid=(bsz, MLA_HEADS // 2, s // tq),
        in_specs=[pl.BlockSpec((1, 2, tq, LANES), lambda b, p, i: (b, p, i, 0)),
                  pl.BlockSpec((1, 2, s, LANES), lambda b, p, i: (b, p, 0, 0)),
                  pl.BlockSpec((1, 1, nc, LANES, tk), lambda b, p, i: (b, p, 0, 0, 0))],
        out_specs=pl.BlockSpec((1, tq, LANES), lambda b, p, i: (b, i, p)),
        out_shape=jax.ShapeDtypeStruct((bsz, s, MLA_HEADS * MLA_V), BF16),
        compiler_params=_params(3), name='mla_attention',
    )(qm, km, vtm)


def _diff_kernel(ctx_chunks, lam_init, q_ref, k_ref, vt_ref, lam_ref, g_ref, o_ref):
    tq = q_ref.shape[1]
    tk = vt_ref.shape[-1]
    pair = pl.program_id(1)
    n_chunks = _key_chunks(pl.program_id(2), ctx_chunks, vt_ref.shape[2])
    lp = lam_ref[...]
    lam = (jnp.exp(jnp.sum(lp[0:1] * lp[1:2], axis=1, keepdims=True))
           - jnp.exp(jnp.sum(lp[2:3] * lp[3:4], axis=1, keepdims=True)) + lam_init)
    q = q_ref[0]
    lane = lax.broadcasted_iota(jnp.int32, q.shape, 1)
    key_chunk = lambda j: k_ref[0, pl.ds(pl.multiple_of(j * tk, tk), tk), :]
    value_t_chunk = lambda j: vt_ref[0, 0, j]
    streams = []
    for hh in range(2):
        for mp in range(2):
            base = ((2 * pair + hh) * 2 + mp) * DIFF_DK
            qsel = jnp.where((lane >= base) & (lane < base + DIFF_DK), q, jnp.zeros_like(q))
            streams.append((qsel, key_chunk, value_t_chunk))
    res = _attend(streams, n_chunks, tq)
    outs = []
    for hh in range(2):
        (a1, l1), (a2, l2) = res[2 * hh], res[2 * hh + 1]
        outs.append(a1 / l1 - lam * (a2 / l2))
    row = lax.broadcasted_iota(jnp.int32, (LANES, tq), 0)
    first = row < DIFF_DV
    ot = jnp.where(first, outs[0], outs[1])
    sq = ot * ot
    ms = jnp.where(first,
                   jnp.sum(sq[0:DIFF_DV], axis=0, keepdims=True),
                   jnp.sum(sq[DIFF_DV:], axis=0, keepdims=True)) * (1.0 / DIFF_DV)
    y = ot * lax.rsqrt(ms + EPS) * g_ref[...] * (1 - lam_init)
    o_ref[0] = y.T.astype(o_ref.dtype)


def _diff_attention(dq, dk, vtd, lam_params, g_col, ctx_chunks, lam_init):
    bsz, s, w = dq.shape
    nc, tk = vtd.shape[2], vtd.shape[4]
    tq = TOKEN_TILE
    return pl.pallas_call(
        functools.partial(_diff_kernel, ctx_chunks, lam_init),
        grid=(bsz, DIFF_HEADS // 2, s // tq),
        in_specs=[pl.BlockSpec((1, tq, w), lambda b, p, i: (b, i, 0)),
                  pl.BlockSpec((1, s, w), lambda b, p, i: (b, 0, 0)),
                  pl.BlockSpec((1, 1, nc, LANES, tk), lambda b, p, i: (b, p, 0, 0, 0)),
                  pl.BlockSpec(lam_params.shape, lambda b, p, i: (0, 0)),
                  pl.BlockSpec(g_col.shape, lambda b, p, i: (0, 0))],
        out_specs=pl.BlockSpec((1, tq, LANES), lambda b, p, i: (b, i, p)),
        out_shape=jax.ShapeDtypeStruct((bsz, s, DIFF_HEADS * DIFF_DV), BF16),
        compiler_params=_params(3), name='diff_attention',
    )(dq, dk, vtd, lam_params, g_col)


def _ref_rows(cum, block, reverse):
    c = cum.shape[0]
    half = block // 2
    parts = []
    for tb in range(c // block):
        r = tb * block + (half if reverse else half - 1)
        parts.append(jnp.broadcast_to(cum[r:r + 1, :], (block, cum.shape[1])))
    return parts[0] if len(parts) == 1 else jnp.concatenate(parts, axis=0)


def _hgrn_kernel(reverse, ctx_chunks, q_ref, v_ref, lf_ref, o_ref):
    c = SCAN_CHUNK
    n_chunks = q_ref.shape[1] // c
    tok = lax.broadcasted_iota(jnp.int32, (c, c), 0)
    src = lax.broadcasted_iota(jnp.int32, (c, c), 1)
    causal = (src >= tok) if reverse else (src <= tok)
    tri = causal.astype(BF16)
    pos_t = (c - 1 - tok) if reverse else tok
    pos_s = (c - 1 - src) if reverse else src
    pos_col = pos_t[:, 0:1]
    lane = lax.broadcasted_iota(jnp.int32, (c, LANES), 1)
    head_a = lane < HGRN_DK
    r2 = lax.broadcasted_iota(jnp.int32, (LANES, LANES), 0)
    c2 = lax.broadcasted_iota(jnp.int32, (LANES, LANES), 1)
    same_head = (r2 < HGRN_DV) == (c2 < HGRN_DK)
    end_row = 0 if reverse else c - 1

    stages = []
    blk = c
    while blk >= SCAN_FINAL_BLOCK:
        shift = blk.bit_length() - 1
        same_block = (pos_t >> shift) == (pos_s >> shift)
        if blk > SCAN_FINAL_BLOCK:
            stages.append((blk, (pos_col & (blk - 1)) >= (blk // 2), jnp.concatenate([same_block] * 2, axis=0)))
        else:
            stages.append((blk, None, jnp.concatenate([same_block & causal] * 2, axis=0)))
        blk //= 2

    def step(i, st):
        if reverse:
            ci = jnp.where(i < ctx_chunks, ctx_chunks - 1 - i, n_chunks - 1 + ctx_chunks - i)
        else:
            ci = i
        off = pl.multiple_of(ci * c, c)
        q = q_ref[0, pl.ds(off, c), :]
        v = v_ref[0, pl.ds(off, c), :]
        lf = lf_ref[0, pl.ds(off, c), :]
        p0 = lf.astype(BF16)
        r0 = lf - p0.astype(F32)
        p1 = r0.astype(BF16)
        p2 = (r0 - p1.astype(F32)).astype(BF16)
        cum = _dot(tri, p0) + _dot(tri, p1) + _dot(tri, p2)
        k = 1.0 - jnp.exp(lf)
        total = cum[end_row:end_row + 1, :]
        o = _dot_nt((q * jnp.exp(cum)).astype(BF16), st.astype(BF16))
        scores = jnp.zeros((2 * c, c), F32)
        for block, later, keep in stages:
            ref = _ref_rows(cum, block, reverse)
            qd = q * jnp.exp(cum - ref)
            kd = k * jnp.exp(ref - cum)
            if later is not None:
                qd = jnp.where(later, qd, 0.0)
                kd = jnp.where(later, 0.0, kd)
            qd2 = jnp.concatenate([jnp.where(head_a, qd, 0.0), jnp.where(head_a, 0.0, qd)], axis=0)
            sc = _dot_nt(qd2.astype(BF16), kd.astype(BF16))
            scores = scores + jnp.where(keep, sc, 0.0)
        o = o + jnp.where(head_a, _dot(scores[0:c].astype(BF16), v), _dot(scores[c:].astype(BF16), v))
        o_ref[0, pl.ds(off, c), :] = o
        ks = (k * jnp.exp(total - cum)).astype(BF16)
        upd = _dot(v.astype(F32).T.astype(BF16), ks)
        return st * jnp.exp(total) + jnp.where(same_head, upd, 0.0)

    lax.fori_loop(0, n_chunks, step, jnp.zeros((LANES, LANES), F32))


def _hgrn_scan(hq, hv, lf, reverse, ctx_chunks):
    bsz, s, w = hq.shape
    pairs = w // LANES
    lf_off = pairs if reverse else 0
    return pl.pallas_call(
        functools.partial(_hgrn_kernel, reverse, ctx_chunks),
        grid=(bsz, pairs),
        in_specs=[pl.BlockSpec((1, s, LANES), lambda b, p: (b, 0, p)),
                  pl.BlockSpec((1, s, LANES), lambda b, p: (b, 0, p)),
                  pl.BlockSpec((1, s, LANES), lambda b, p: (b, 0, p + lf_off))],
        out_specs=pl.BlockSpec((1, s, LANES), lambda b, p: (b, 0, p)),
        out_shape=jax.ShapeDtypeStruct((bsz, s, w), F32),
        compiler_params=_params(2), name='hgrn_bwd' if reverse else 'hgrn_fwd',
    )(hq, hv, lf)


def _post_kernel(ffn_chunk, x_ref, mod_ref, mla_ref, diff_ref, of_ref, ob_ref, hg_ref, gh_ref, seg_ref,
                 wo_ref, g2_ref, wg_ref, wu_ref, wd_ref, o_ref):
    d = x_ref.shape[-1]
    x = x_ref[0]
    mod = mod_ref[0]
    gt1, sh2, sc2, gt2 = (mod[:, 2 * d:3 * d], mod[:, 3 * d:4 * d], mod[:, 4 * d:5 * d], mod[:, 5 * d:6 * d])
    o = of_ref[0] + ob_ref[0]
    sq = o * o
    sq_hi = sq.astype(BF16)
    sq_lo = (sq - sq_hi.astype(F32)).astype(BF16)
    seg = seg_ref[...]
    ms = (_dot(sq_hi, seg) + _dot(sq_lo, seg)) * (1.0 / HGRN_DV)
    hn = (o * lax.rsqrt(ms + EPS) * gh_ref[...] * _silu(hg_ref[0])).astype(BF16)
    na, nb = mla_ref.shape[-1], diff_ref.shape[-1]
    y = (_dot(mla_ref[0], wo_ref[0:na, :]) + _dot(diff_ref[0], wo_ref[na:na + nb, :])
         + _dot(hn, wo_ref[na + nb:, :]))
    x1 = x + gt1 * y
    h2 = (_rms(x1, g2_ref[...]) * (1 + sc2) + sh2).astype(BF16)
    dff = wg_ref.shape[1]
    acc = jnp.zeros_like(x1)
    for lo in range(0, dff, ffn_chunk):
        g = _dot(h2, wg_ref[:, lo:lo + ffn_chunk])
        u = _dot(h2, wu_ref[:, lo:lo + ffn_chunk])
        acc = acc + _dot((_silu(g) * u).astype(BF16), wd_ref[lo:lo + ffn_chunk, :])
    o_ref[0] = x1 + gt2 * acc


def _post_mixer(xall, mods3, ctx_tiles, mla, diff, o_f, o_b, hg, gh, seg, wo, g2, wg, wu, wd):
    bsz, s, d = xall.shape
    tm = TOKEN_TILE
    dff = wg.shape[1]
    ffn_chunk = dff // 2
    tok = lambda w: pl.BlockSpec((1, tm, w), lambda b, i: (b, i, 0))
    in_specs = [
        tok(d),
        _mod_spec(mods3, ctx_tiles),
        tok(mla.shape[-1]), tok(diff.shape[-1]), tok(HGRN_WIDTH), tok(HGRN_WIDTH), tok(HGRN_WIDTH),
        _const_spec(gh.shape), _const_spec(seg.shape), _const_spec(wo.shape), _const_spec(g2.shape),
        _const_spec(wg.shape), _const_spec(wu.shape), _const_spec(wd.shape),
    ]
    return pl.pallas_call(
        functools.partial(_post_kernel, ffn_chunk),
        grid=(bsz, s // tm), in_specs=in_specs, out_specs=tok(d),
        out_shape=jax.ShapeDtypeStruct((bsz, s, d), F32),
        compiler_params=_params(2), name='post_mixer',
    )(xall, mods3, mla, diff, o_f, o_b, hg, gh, seg, wo, g2, wg, wu, wd)


def _final_kernel(x_ref, g_ref, o_ref):
    o_ref[0] = _rms(x_ref[0], g_ref[...])


def _final_norm(xall, g, ctx_tiles, seq):
    bsz, _, d = xall.shape
    tm = TOKEN_TILE
    return pl.pallas_call(
        _final_kernel,
        grid=(bsz, seq // tm),
        in_specs=[pl.BlockSpec((1, tm, d), lambda b, i: (b, i + ctx_tiles, 0)),
                  pl.BlockSpec(g.shape, lambda b, i: (0, 0))],
        out_specs=pl.BlockSpec((1, tm, d), lambda b, i: (b, i, 0)),
        out_shape=jax.ShapeDtypeStruct((bsz, seq, d), F32),
        compiler_params=_params(2), name='final_norm',
    )(xall, g)


def _split_cols(w, sizes):
    return jnp.split(w, np.cumsum(sizes)[:-1].tolist(), axis=-1)


def _swap32(w):
    lead = w.shape[:-1]
    return w.reshape(*lead, -1, ROPE_DIM)[..., _ROPE_SWAP].reshape(*lead, -1)


def _layout_weights(w_in, w_uq, w_ukv):
    depth, d, _ = w_in.shape
    in_sizes = (MLA_Q_RANK, MLA_KV_RANK, MLA_ROPE, 256, 256, 256,
                HGRN_WIDTH, HGRN_WIDTH, HGRN_WIDTH, HGRN_WIDTH, HGRN_WIDTH)
    cq, ckv, kr, dq, dk, dv, hq, hff, hfb, hi, hg = _split_cols(w_in, in_sizes)
    zeros = lambda rows, n: jnp.zeros((depth, rows, n), F32)
    rope_tile = lambda w: jnp.concatenate([zeros(w.shape[1], MLA_NOPE), w, zeros(w.shape[1], LANES - MLA_NOPE - MLA_ROPE)], -1)
    dqs = dq * DIFF_SCALE
    win = jnp.concatenate([cq, ckv, rope_tile(kr), rope_tile(_swap32(kr)), dqs, _swap32(dqs), dk, _swap32(dk), dv,
                           hq, hff, hfb, hi, hg], axis=-1).astype(BF16)
    per = MLA_NOPE + MLA_ROPE
    uq_tiles, uq_sw_tiles = [], []
    for h in range(MLA_HEADS):
        nope = w_uq[..., h * per:h * per + MLA_NOPE]
        rope = w_uq[..., h * per + MLA_NOPE:(h + 1) * per]
        uq_tiles.append(jnp.concatenate([nope, rope, zeros(MLA_Q_RANK, LANES - per)], -1))
        uq_sw_tiles.append(rope_tile(_swap32(rope)))
    wuq = (jnp.concatenate(uq_tiles + uq_sw_tiles, axis=-1) * MLA_SCALE).astype(BF16)
    kv = MLA_NOPE + MLA_V
    k_tiles = [jnp.concatenate([w_ukv[..., h * kv:h * kv + MLA_NOPE], zeros(MLA_KV_RANK, LANES - MLA_NOPE)], -1)
               for h in range(MLA_HEADS)]
    v_cols = [w_ukv[..., h * kv + MLA_NOPE:(h + 1) * kv] for h in range(MLA_HEADS)]
    wukv = jnp.concatenate(k_tiles + v_cols, axis=-1).astype(BF16)
    return win, wuq, wukv


def _rope_tables(seq, ctx_len):
    t = jnp.arange(seq)
    row = (t // GRID_W).astype(F32)
    col = (t % GRID_W).astype(F32)
    n_freq = ROPE_DIM // 4
    freqs = ROPE_BASE ** (-jnp.arange(n_freq, dtype=F32) / n_freq)
    ang_r = row[:, None] * freqs
    ang_c = col[:, None] * freqs
    cr, sr, cc, sc = jnp.cos(ang_r), jnp.sin(ang_r), jnp.cos(ang_c), jnp.sin(ang_c)
    c32 = jnp.concatenate([cr, cr, cc, cc], axis=-1)
    s32 = jnp.concatenate([-sr, sr, -sc, sc], axis=-1)
    c32 = jnp.concatenate([jnp.ones((ctx_len, ROPE_DIM), F32), c32], axis=0)
    s32 = jnp.concatenate([jnp.zeros((ctx_len, ROPE_DIM), F32), s32], axis=0)
    n = c32.shape[0]
    cd = jnp.tile(c32, (1, LANES // ROPE_DIM))
    sd = jnp.tile(s32, (1, LANES // ROPE_DIM))
    pad = jnp.zeros((n, LANES - MLA_NOPE - MLA_ROPE), F32)
    cq = jnp.concatenate([jnp.ones((n, MLA_NOPE), F32), c32, pad], axis=-1)
    sq = jnp.concatenate([jnp.zeros((n, MLA_NOPE), F32), s32, pad], axis=-1)
    return cd, sd, cq, sq


def kernel(x, c, ctx, c_ctx, w_ada, b_ada, g_norm1, g_norm2, w_in, g_q_norm, w_uq, g_kv_norm, w_ukv,
           diff_lambda, g_diff_norm, hgrn_lower_bounds, g_hgrn_norm, w_out, w_ffn_gate, w_ffn_up,
           w_ffn_down, g_final):
    bsz, seq, d = x.shape
    ctx_len = ctx.shape[1]
    depth = w_in.shape[0]
    assert ctx_len % TOKEN_TILE == 0 and seq % TOKEN_TILE == 0 and seq % GRID_W == 0
    ctx_tiles = ctx_len // TOKEN_TILE
    ctx_scan_chunks = ctx_len // SCAN_CHUNK

    cvec = jnp.concatenate([c, c_ctx[None, :], jnp.zeros((8 - bsz - 1, d), F32)], axis=0)
    mods = _adaln(cvec, w_ada, b_ada)
    win, wuq, wukv = _layout_weights(w_in, w_uq, w_ukv)
    wo, wg, wu, wd = (w.astype(BF16) for w in (w_out, w_ffn_gate, w_ffn_up, w_ffn_down))
    tables = _rope_tables(seq, ctx_len)

    p = jax.nn.softmax(hgrn_lower_bounds.astype(F32), axis=0)
    cum = jnp.cumsum(p, axis=0)
    lb = (cum - cum[0:1]).reshape(depth, 1, 2 * HGRN_WIDTH)
    llb, l1mlb = jnp.log(lb), jnp.log1p(-lb)

    idx = np.arange(HGRN_WIDTH)
    seg = jnp.asarray(idx[:, None] // HGRN_DV == idx[None, :] // HGRN_DV, BF16)

    xall = jnp.concatenate([ctx, x], axis=1)
    for l in range(depth):
        lam_init = 0.8 - 0.6 * math.exp(-0.3 * l)
        mods3 = mods[l, 0:bsz + 1].reshape(bsz + 1, 1, 6 * d)
        qm, km, vtm, dq, dk, vtd, hq, hv, lf, hg = _pre_mixer(
            xall, mods3, ctx_tiles, g_norm1[l][None], win[l], g_q_norm[l][None], wuq[l], g_kv_norm[l][None], wukv[l],
            llb[l], l1mlb[l], tables)
        mla = _mla_attention(qm, km, vtm, ctx_tiles)
        g_col = jnp.tile(g_diff_norm[l], 2)[:, None]
        diff = _diff_attention(dq, dk, vtd, diff_lambda[l].astype(F32), g_col, ctx_tiles, lam_init)
        o_f = _hgrn_scan(hq, hv, lf, False, ctx_scan_chunks)
        o_b = _hgrn_scan(hq, hv, lf, True, ctx_scan_chunks)
        gh = jnp.tile(g_hgrn_norm[l], HGRN_HEADS)[None]
        xall = _post_mixer(xall, mods3, ctx_tiles, mla, diff, o_f, o_b, hg, gh, seg, wo[l], g_norm2[l][None],
                           wg[l], wu[l], wd[l])
    return _final_norm(xall, g_final[None], ctx_tiles, seq)
```

```python
import functools
import math

import jax
import jax.numpy as jnp
import numpy as np
from jax import lax
from jax.experimental import pallas as pl
from jax.experimental.pallas import tpu as pltpu

F32 = jnp.float32
BF16 = jnp.bfloat16

GRID_W = 64
EPS = 1e-6
ROPE_BASE = 10000.0
ROPE_DIM = 32

MLA_HEADS = 4
MLA_Q_RANK = 256
MLA_KV_RANK = 128
MLA_NOPE = 64
MLA_ROPE = ROPE_DIM
MLA_V = 64
MLA_SCALE = (MLA_NOPE + MLA_ROPE) ** -0.5

DIFF_HEADS = 4
DIFF_DK = ROPE_DIM
DIFF_DV = 2 * DIFF_DK
DIFF_SCALE = DIFF_DK ** -0.5
LOG2_E = math.log2(math.e)

HGRN_HEADS = 8
HGRN_DK = 64
HGRN_DV = 64
HGRN_WIDTH = HGRN_HEADS * HGRN_DV

LANES = 128
MXU_DIM = 256
VMEM_LIMIT_BYTES = 56 * 1024 * 1024

TOKEN_TILE = MXU_DIM
SCAN_CHUNK = 64
SCAN_FINAL_BLOCK = 16

_IN_GROUPS = (
    ('cq', MLA_Q_RANK), ('ckv', MLA_KV_RANK), ('kr', LANES), ('kr_sw', LANES),
    ('dq', 256), ('dq_sw', 256), ('dk', 256), ('dk_sw', 256), ('dv', 256),
    ('hq', HGRN_WIDTH), ('hf', 2 * HGRN_WIDTH), ('hi', HGRN_WIDTH), ('hg', HGRN_WIDTH),
)
_IN_OFF = {}
_o = 0
for _n, _w in _IN_GROUPS:
    _IN_OFF[_n] = (_o, _o + _w)
    _o += _w
IN_WIDTH_PADDED = _o

_ROPE_SWAP = np.concatenate([np.arange(8, 16), np.arange(0, 8), np.arange(24, 32), np.arange(16, 24)])


def _silu(x):
    return x * jax.nn.sigmoid(x)


def _rms(x, g):
    return x * lax.rsqrt(jnp.mean(x * x, axis=-1, keepdims=True) + EPS) * g


def _dot(a, b):
    return jnp.dot(a, b, preferred_element_type=F32)


def _dot_nt(a, b):
    return lax.dot_general(a, b, (((1,), (1,)), ((), ())), preferred_element_type=F32)


def _params(n_grid):
    return pltpu.CompilerParams(dimension_semantics=('arbitrary',) * n_grid,
                                vmem_limit_bytes=VMEM_LIMIT_BYTES)


def _const_spec(shape):
    nd = len(shape)
    return pl.BlockSpec(shape, lambda *_: (0,) * nd, pipeline_mode=pl.Buffered(1))


def _adaln_kernel(cv_ref, w_ref, b_ref, o_ref):
    s = _silu(cv_ref[...])
    o_ref[0] = jnp.dot(s, w_ref[0], preferred_element_type=F32,
                       precision=lax.Precision.HIGHEST) + b_ref[0]


def _adaln(cvec, w_ada, b_ada):
    depth, d, n = w_ada.shape
    tn = 1536
    return pl.pallas_call(
        _adaln_kernel,
        grid=(depth, n // tn),
        in_specs=[pl.BlockSpec((8, d), lambda l, j: (0, 0)),
                  pl.BlockSpec((1, d, tn), lambda l, j: (l, 0, j)),
                  pl.BlockSpec((1, 1, tn), lambda l, j: (l, 0, j))],
        out_specs=pl.BlockSpec((1, 8, tn), lambda l, j: (l, 0, j)),
        out_shape=jax.ShapeDtypeStruct((depth, 8, n), F32),
        compiler_params=_params(2),
        name='adaln',
    )(cvec, w_ada, b_ada.reshape(depth, 1, n))


def _pre_kernel(x_ref, mod_ref, g1_ref, win_ref, gq_ref, wuq_ref, gkv_ref, wukv_ref, llb_ref, l1mlb_ref,
                cd_ref, sd_ref, cq_ref, sq_ref,
                qm_ref, km_ref, vtm_ref, dq_ref, dk_ref, vtd_ref, hq_ref, hv_ref, lf_ref, hg_ref):
    d = x_ref.shape[-1]
    x = x_ref[0]
    mod = mod_ref[0]
    sh1, sc1 = mod[:, 0:d], mod[:, d:2 * d]
    hb = (_rms(x, g1_ref[...]) * (1 + sc1) + sh1).astype(BF16)

    def proj(name):
        lo, hi = _IN_OFF[name]
        return _dot(hb, win_ref[:, lo:hi])

    cq_t, sq_t = cq_ref[...], sq_ref[...]
    cqn = _rms(proj('cq'), gq_ref[...]).astype(BF16)
    uq = _dot(cqn, wuq_ref[...])
    for h in range(MLA_HEADS):
        a = uq[:, h * LANES:(h + 1) * LANES]
        b = uq[:, (MLA_HEADS + h) * LANES:(MLA_HEADS + h + 1) * LANES]
        qm_ref[0, h] = (a * cq_t + b * sq_t).astype(BF16)
    ckvn = _rms(proj('ckv'), gkv_ref[...]).astype(BF16)
    ukv = _dot(ckvn, wukv_ref[...])
    krt = proj('kr') * cq_t + proj('kr_sw') * sq_t
    for h in range(MLA_HEADS):
        km_ref[0, h] = (ukv[:, h * LANES:(h + 1) * LANES] + krt).astype(BF16)
    for p in range(MLA_HEADS // 2):
        lo = (MLA_HEADS + p) * LANES
        vtm_ref[0, p, 0] = ukv[:, lo:lo + LANES].T.astype(BF16)
    cd2 = jnp.concatenate([cd_ref[...]] * 2, axis=1)
    sd2 = jnp.concatenate([sd_ref[...]] * 2, axis=1)
    dq_ref[0] = (proj('dq') * cd2 + proj('dq_sw') * sd2).astype(BF16)
    dk_ref[0] = (proj('dk') * cd2 + proj('dk_sw') * sd2).astype(BF16)
    dv = proj('dv')
    for p in range(DIFF_HEADS // 2):
        vtd_ref[0, p, 0] = dv[:, p * LANES:(p + 1) * LANES].T.astype(BF16)
    hq_ref[0] = _silu(proj('hq'))
    z = proj('hf')
    ls = jnp.minimum(z, 0.0) - jnp.log1p(jnp.exp(-jnp.abs(z)))
    a = llb_ref[...]
    b = l1mlb_ref[...] + ls
    lf_ref[0] = jnp.maximum(a, b) + jnp.log1p(jnp.exp(-jnp.abs(a - b)))
    hv_ref[0] = proj('hi').astype(BF16)
    hg_ref[0] = proj('hg')


def _mod_spec(mods3, ctx_tiles):
    ctx_row = mods3.shape[0] - 1
    return pl.BlockSpec((1, 1, mods3.shape[-1]), lambda b, i: (jnp.where(i < ctx_tiles, ctx_row, b), 0, 0))


def _pre_mixer(xall, mods3, ctx_tiles, g1, win, gq, wuq, gkv, wukv, llb, l1mlb, tables):
    bsz, s, d = xall.shape
    tm = TOKEN_TILE
    ns = s // tm
    grid = (bsz, ns)
    tok = lambda w: pl.BlockSpec((1, tm, w), lambda b, i: (b, i, 0))
    tab = pl.BlockSpec((tm, LANES), lambda b, i: (i, 0))
    head4 = pl.BlockSpec((1, MLA_HEADS, tm, LANES), lambda b, i: (b, 0, i, 0))
    vt = pl.BlockSpec((1, 2, 1, LANES, tm), lambda b, i: (b, 0, i, 0, 0))
    in_specs = [
        tok(d),
        _mod_spec(mods3, ctx_tiles),
        _const_spec(g1.shape), _const_spec(win.shape), _const_spec(gq.shape), _const_spec(wuq.shape),
        _const_spec(gkv.shape), _const_spec(wukv.shape), _const_spec(llb.shape), _const_spec(l1mlb.shape),
        tab, tab, tab, tab,
    ]
    sds = jax.ShapeDtypeStruct
    out_shape = (
        sds((bsz, MLA_HEADS, s, LANES), BF16), sds((bsz, MLA_HEADS, s, LANES), BF16),
        sds((bsz, 2, ns, LANES, tm), BF16),
        sds((bsz, s, 256), BF16), sds((bsz, s, 256), BF16), sds((bsz, 2, ns, LANES, tm), BF16),
        sds((bsz, s, HGRN_WIDTH), F32), sds((bsz, s, HGRN_WIDTH), BF16),
        sds((bsz, s, 2 * HGRN_WIDTH), F32), sds((bsz, s, HGRN_WIDTH), F32),
    )
    out_specs = (head4, head4, vt, tok(256), tok(256), vt,
                 tok(HGRN_WIDTH), tok(HGRN_WIDTH), tok(2 * HGRN_WIDTH), tok(HGRN_WIDTH))
    return pl.pallas_call(
        _pre_kernel, grid=grid, in_specs=in_specs, out_specs=out_specs, out_shape=out_shape,
        compiler_params=_params(2), name='pre_mixer',
    )(xall, mods3, g1, win, gq, wuq, gkv, wukv, llb, l1mlb, *tables)


def _attend(streams, n_groups, tkg, s_ref):
    tq = s_ref.shape[-1]

    def score(g, slot):
        for si, (q, key_rows, _) in enumerate(streams):
            s_ref[si, slot, 0:tkg, :] = _dot_nt(key_rows(g), q)

    def softmax_pv(g, slot, carry):
        new = []
        for si, ((_, _, value_t), (m, l, acc)) in enumerate(zip(streams, carry)):
            s = s_ref[si, slot, 0:tkg, :]
            m_new = jnp.maximum(m, jnp.max(s, axis=0, keepdims=True))
            alpha = jnp.exp2(m - m_new)
            p = jnp.exp2(s - m_new)
            l = alpha * l + jnp.sum(p, axis=0, keepdims=True)
            acc = alpha * acc + _dot(value_t(g), p.astype(BF16))
            new.append((m_new, l, acc))
        return tuple(new)

    def body(t, carry):
        g = 2 * t
        score(g + 1, 1)
        carry = softmax_pv(g, 0, carry)
        score(g + 2, 0)
        return softmax_pv(g + 1, 1, carry)

    carry = tuple((jnp.full((1, tq), -jnp.inf, F32), jnp.zeros((1, tq), F32), jnp.zeros((LANES, tq), F32))
                  for _ in streams)
    score(0, 0)
    n_pairs = (n_groups - 1) // 2
    if n_pairs > 0:
        carry = lax.fori_loop(0, n_pairs, body, carry)
    g = 2 * n_pairs
    if n_groups % 2 == 0:
        score(g + 1, 1)
        carry = softmax_pv(g, 0, carry)
        carry = softmax_pv(g + 1, 1, carry)
    else:
        carry = softmax_pv(g, 0, carry)
    return [(acc, l) for (_, l, acc) in carry]


def _chunks_per_group(n_chunks):
    return next(u for u in (3, 4, 2, 1) if n_chunks % u == 0)


def _for_context_and_latent(i, ctx_chunks, all_chunks, run):
    @pl.when(i < ctx_chunks)
    def _():
        u = _chunks_per_group(ctx_chunks)
        run(ctx_chunks // u, u)

    @pl.when(i >= ctx_chunks)
    def _():
        u = _chunks_per_group(all_chunks)
        run(all_chunks // u, u)


def _key_rows(k_ref, lead, tk, u):
    return lambda g: k_ref[lead + (pl.ds(pl.multiple_of(g * (u * tk), u * tk), u * tk), slice(None))]


def _value_t(vt_ref, u):
    def get(g):
        parts = [vt_ref[0, 0, g * u + c] for c in range(u)]
        return parts[0] if u == 1 else jnp.concatenate(parts, axis=1)
    return get


def _score_scratch(n_streams, ctx_chunks, all_chunks, tk, tq):
    rows = max(_chunks_per_group(ctx_chunks), _chunks_per_group(all_chunks)) * tk
    return pltpu.VMEM((n_streams, 2, rows, tq), F32)


def _mla_kernel(ctx_chunks, q_ref, k_ref, vt_ref, o_ref, s_ref):
    tq = q_ref.shape[2]
    tk = vt_ref.shape[-1]

    def run(n_groups, u):
        streams = [(q_ref[0, hh], _key_rows(k_ref, (0, hh), tk, u), _value_t(vt_ref, u)) for hh in range(2)]
        (acc0, l0), (acc1, l1) = _attend(streams, n_groups, u * tk, s_ref)
        row = lax.broadcasted_iota(jnp.int32, (LANES, tq), 0)
        ot = jnp.where(row < MLA_V, acc0 / l0, acc1 / l1)
        o_ref[0] = ot.T.astype(o_ref.dtype)

    _for_context_and_latent(pl.program_id(2), ctx_chunks, vt_ref.shape[2], run)


def _mla_attention(qm, km, vtm, ctx_chunks):
    bsz, _, s, _ = qm.shape
    nc, tk = vtm.shape[2], vtm.shape[4]
    tq = TOKEN_TILE
    return pl.pallas_call(
        functools.partial(_mla_kernel, ctx_chunks),
        grid=(bsz, MLA_HEADS // 2, s // tq),
        in_specs=[pl.BlockSpec((1, 2, tq, LANES), lambda b, p, i: (b, p, i, 0)),
                  pl.BlockSpec((1, 2, s, LANES), lambda b, p, i: (b, p, 0, 0)),
                  pl.BlockSpec((1, 1, nc, LANES, tk), lambda b, p, i: (b, p, 0, 0, 0))],
        out_specs=pl.BlockSpec((1, tq, LANES), lambda b, p, i: (b, i, p)),
        out_shape=jax.ShapeDtypeStruct((bsz, s, MLA_HEADS * MLA_V), BF16),
        scratch_shapes=[_score_scratch(2, ctx_chunks, nc, tk, tq)],
        compiler_params=_params(3), name='mla_attention',
    )(qm, km, vtm)


def _diff_kernel(ctx_chunks, lam_init, q_ref, k_ref, vt_ref, lam_ref, g_ref, o_ref, s_ref):
    tq = q_ref.shape[1]
    tk = vt_ref.shape[-1]
    pair = pl.program_id(1)

    def run(n_groups, u):
        lp = lam_ref[...]
        lam = (jnp.exp(jnp.sum(lp[0:1] * lp[1:2], axis=1, keepdims=True))
               - jnp.exp(jnp.sum(lp[2:3] * lp[3:4], axis=1, keepdims=True)) + lam_init)
        q = q_ref[0]
        lane = lax.broadcasted_iota(jnp.int32, q.shape, 1)
        key_rows, value_t = _key_rows(k_ref, (0,), tk, u), _value_t(vt_ref, u)
        streams = []
        for hh in range(2):
            for mp in range(2):
                base = ((2 * pair + hh) * 2 + mp) * DIFF_DK
                qsel = jnp.where((lane >= base) & (lane < base + DIFF_DK), q, jnp.zeros_like(q))
                streams.append((qsel, key_rows, value_t))
        res = _attend(streams, n_groups, u * tk, s_ref)
        outs = []
        for hh in range(2):
            (a1, l1), (a2, l2) = res[2 * hh], res[2 * hh + 1]
            outs.append(a1 / l1 - lam * (a2 / l2))
        row = lax.broadcasted_iota(jnp.int32, (LANES, tq), 0)
        first = row < DIFF_DV
        ot = jnp.where(first, outs[0], outs[1])
        sq = ot * ot
        ms = jnp.where(first,
                       jnp.sum(sq[0:DIFF_DV], axis=0, keepdims=True),
                       jnp.sum(sq[DIFF_DV:], axis=0, keepdims=True)) * (1.0 / DIFF_DV)
        y = ot * lax.rsqrt(ms + EPS) * g_ref[...] * (1 - lam_init)
        o_ref[0] = y.T.astype(o_ref.dtype)

    _for_context_and_latent(pl.program_id(2), ctx_chunks, vt_ref.shape[2], run)


def _diff_attention(dq, dk, vtd, lam_params, g_col, ctx_chunks, lam_init):
    bsz, s, w = dq.shape
    nc, tk = vtd.shape[2], vtd.shape[4]
    tq = TOKEN_TILE
    return pl.pallas_call(
        functools.partial(_diff_kernel, ctx_chunks, lam_init),
        grid=(bsz, DIFF_HEADS // 2, s // tq),
        in_specs=[pl.BlockSpec((1, tq, w), lambda b, p, i: (b, i, 0)),
                  pl.BlockSpec((1, s, w), lambda b, p, i: (b, 0, 0)),
                  pl.BlockSpec((1, 1, nc, LANES, tk), lambda b, p, i: (b, p, 0, 0, 0)),
                  pl.BlockSpec(lam_params.shape, lambda b, p, i: (0, 0)),
                  pl.BlockSpec(g_col.shape, lambda b, p, i: (0, 0))],
        out_specs=pl.BlockSpec((1, tq, LANES), lambda b, p, i: (b, i, p)),
        out_shape=jax.ShapeDtypeStruct((bsz, s, DIFF_HEADS * DIFF_DV), BF16),
        scratch_shapes=[_score_scratch(4, ctx_chunks, nc, tk, tq)],
        compiler_params=_params(3), name='diff_attention',
    )(dq, dk, vtd, lam_params, g_col)


def _ref_rows(cum, block, reverse):
    c = cum.shape[0]
    half = block // 2
    parts = []
    for tb in range(c // block):
        r = tb * block + (half if reverse else half - 1)
        parts.append(jnp.broadcast_to(cum[r:r + 1, :], (block, cum.shape[1])))
    return parts[0] if len(parts) == 1 else jnp.concatenate(parts, axis=0)


def _hgrn_kernel(reverse, ctx_chunks, q_ref, v_ref, lf_ref, o_ref):
    c = SCAN_CHUNK
    n_chunks = q_ref.shape[1] // c
    tok = lax.broadcasted_iota(jnp.int32, (c, c), 0)
    src = lax.broadcasted_iota(jnp.int32, (c, c), 1)
    causal = (src >= tok) if reverse else (src <= tok)
    tri = causal.astype(BF16)
    pos_t = (c - 1 - tok) if reverse else tok
    pos_s = (c - 1 - src) if reverse else src
    pos_col = pos_t[:, 0:1]
    lane = lax.broadcasted_iota(jnp.int32, (c, LANES), 1)
    head_a = lane < HGRN_DK
    r2 = lax.broadcasted_iota(jnp.int32, (LANES, LANES), 0)
    c2 = lax.broadcasted_iota(jnp.int32, (LANES, LANES), 1)
    same_head = (r2 < HGRN_DV) == (c2 < HGRN_DK)
    end_row = 0 if reverse else c - 1

    stages = []
    blk = c
    while blk >= SCAN_FINAL_BLOCK:
        shift = blk.bit_length() - 1
        same_block = (pos_t >> shift) == (pos_s >> shift)
        if blk > SCAN_FINAL_BLOCK:
            stages.append((blk, (pos_col & (blk - 1)) >= (blk // 2), jnp.concatenate([same_block] * 2, axis=0)))
        else:
            stages.append((blk, None, jnp.concatenate([same_block & causal] * 2, axis=0)))
        blk //= 2

    def step(i, st):
        if reverse:
            ci = jnp.where(i < ctx_chunks, ctx_chunks - 1 - i, n_chunks - 1 + ctx_chunks - i)
        else:
            ci = i
        off = pl.multiple_of(ci * c, c)
        q = q_ref[0, pl.ds(off, c), :]
        v = v_ref[0, pl.ds(off, c), :]
        lf = lf_ref[0, pl.ds(off, c), :]
        p0 = lf.astype(BF16)
        r0 = lf - p0.astype(F32)
        p1 = r0.astype(BF16)
        p2 = (r0 - p1.astype(F32)).astype(BF16)
        cum = _dot(tri, p0) + _dot(tri, p1) + _dot(tri, p2)
        k = 1.0 - jnp.exp(lf)
        total = cum[end_row:end_row + 1, :]
        o = _dot_nt((q * jnp.exp(cum)).astype(BF16), st.astype(BF16))
        scores = jnp.zeros((2 * c, c), F32)
        for block, later, keep in stages:
            ref = _ref_rows(cum, block, reverse)
            qd = q * jnp.exp(cum - ref)
            kd = k * jnp.exp(ref - cum)
            if later is not None:
                qd = jnp.where(later, qd, 0.0)
                kd = jnp.where(later, 0.0, kd)
            qd2 = jnp.concatenate([jnp.where(head_a, qd, 0.0), jnp.where(head_a, 0.0, qd)], axis=0)
            sc = _dot_nt(qd2.astype(BF16), kd.astype(BF16))
            scores = scores + jnp.where(keep, sc, 0.0)
        o = o + jnp.where(head_a, _dot(scores[0:c].astype(BF16), v), _dot(scores[c:].astype(BF16), v))
        o_ref[0, pl.ds(off, c), :] = o
        ks = (k * jnp.exp(total - cum)).astype(BF16)
        upd = _dot(v.astype(F32).T.astype(BF16), ks)
        return st * jnp.exp(total) + jnp.where(same_head, upd, 0.0)

    lax.fori_loop(0, n_chunks, step, jnp.zeros((LANES, LANES), F32))


def _hgrn_scan(hq, hv, lf, reverse, ctx_chunks):
    bsz, s, w = hq.shape
    pairs = w // LANES
    lf_off = pairs if reverse else 0
    return pl.pallas_call(
        functools.partial(_hgrn_kernel, reverse, ctx_chunks),
        grid=(bsz, pairs),
        in_specs=[pl.BlockSpec((1, s, LANES), lambda b, p: (b, 0, p)),
                  pl.BlockSpec((1, s, LANES), lambda b, p: (b, 0, p)),
                  pl.BlockSpec((1, s, LANES), lambda b, p: (b, 0, p + lf_off))],
        out_specs=pl.BlockSpec((1, s, LANES), lambda b, p: (b, 0, p)),
        out_shape=jax.ShapeDtypeStruct((bsz, s, w), F32),
        compiler_params=_params(2), name='hgrn_bwd' if reverse else 'hgrn_fwd',
    )(hq, hv, lf)


def _post_kernel(ffn_chunk, x_ref, mod_ref, mla_ref, diff_ref, of_ref, ob_ref, hg_ref, gh_ref, seg_ref,
                 wo_ref, g2_ref, wg_ref, wu_ref, wd_ref, o_ref):
    d = x_ref.shape[-1]
    x = x_ref[0]
    mod = mod_ref[0]
    gt1, sh2, sc2, gt2 = (mod[:, 2 * d:3 * d], mod[:, 3 * d:4 * d], mod[:, 4 * d:5 * d], mod[:, 5 * d:6 * d])
    o = of_ref[0] + ob_ref[0]
    sq = o * o
    sq_hi = sq.astype(BF16)
    sq_lo = (sq - sq_hi.astype(F32)).astype(BF16)
    seg = seg_ref[...]
    ms = (_dot(sq_hi, seg) + _dot(sq_lo, seg)) * (1.0 / HGRN_DV)
    hn = (o * lax.rsqrt(ms + EPS) * gh_ref[...] * _silu(hg_ref[0])).astype(BF16)
    na, nb = mla_ref.shape[-1], diff_ref.shape[-1]
    y = (_dot(mla_ref[0], wo_ref[0:na, :]) + _dot(diff_ref[0], wo_ref[na:na + nb, :])
         + _dot(hn, wo_ref[na + nb:, :]))
    x1 = x + gt1 * y
    h2 = (_rms(x1, g2_ref[...]) * (1 + sc2) + sh2).astype(BF16)
    dff = wg_ref.shape[1]
    acc = jnp.zeros_like(x1)
    for lo in range(0, dff, ffn_chunk):
        g = _dot(h2, wg_ref[:, lo:lo + ffn_chunk])
        u = _dot(h2, wu_ref[:, lo:lo + ffn_chunk])
        acc = acc + _dot((_silu(g) * u).astype(BF16), wd_ref[lo:lo + ffn_chunk, :])
    o_ref[0] = x1 + gt2 * acc


def _post_mixer(xall, mods3, ctx_tiles, mla, diff, o_f, o_b, hg, gh, seg, wo, g2, wg, wu, wd):
    bsz, s, d = xall.shape
    tm = TOKEN_TILE
    dff = wg.shape[1]
    ffn_chunk = dff // 2
    tok = lambda w: pl.BlockSpec((1, tm, w), lambda b, i: (b, i, 0))
    in_specs = [
        tok(d),
        _mod_spec(mods3, ctx_tiles),
        tok(mla.shape[-1]), tok(diff.shape[-1]), tok(HGRN_WIDTH), tok(HGRN_WIDTH), tok(HGRN_WIDTH),
        _const_spec(gh.shape), _const_spec(seg.shape), _const_spec(wo.shape), _const_spec(g2.shape),
        _const_spec(wg.shape), _const_spec(wu.shape), _const_spec(wd.shape),
    ]
    return pl.pallas_call(
        functools.partial(_post_kernel, ffn_chunk),
        grid=(bsz, s // tm), in_specs=in_specs, out_specs=tok(d),
        out_shape=jax.ShapeDtypeStruct((bsz, s, d), F32),
        compiler_params=_params(2), name='post_mixer',
    )(xall, mods3, mla, diff, o_f, o_b, hg, gh, seg, wo, g2, wg, wu, wd)


def _final_kernel(x_ref, g_ref, o_ref):
    o_ref[0] = _rms(x_ref[0], g_ref[...])


def _final_norm(xall, g, ctx_tiles, seq):
    bsz, _, d = xall.shape
    tm = TOKEN_TILE
    return pl.pallas_call(
        _final_kernel,
        grid=(bsz, seq // tm),
        in_specs=[pl.BlockSpec((1, tm, d), lambda b, i: (b, i + ctx_tiles, 0)),
                  pl.BlockSpec(g.shape, lambda b, i: (0, 0))],
        out_specs=pl.BlockSpec((1, tm, d), lambda b, i: (b, i, 0)),
        out_shape=jax.ShapeDtypeStruct((bsz, seq, d), F32),
        compiler_params=_params(2), name='final_norm',
    )(xall, g)


def _split_cols(w, sizes):
    return jnp.split(w, np.cumsum(sizes)[:-1].tolist(), axis=-1)


def _swap32(w):
    lead = w.shape[:-1]
    return w.reshape(*lead, -1, ROPE_DIM)[..., _ROPE_SWAP].reshape(*lead, -1)


def _layout_weights(w_in, w_uq, w_ukv):
    depth, d, _ = w_in.shape
    in_sizes = (MLA_Q_RANK, MLA_KV_RANK, MLA_ROPE, 256, 256, 256,
                HGRN_WIDTH, HGRN_WIDTH, HGRN_WIDTH, HGRN_WIDTH, HGRN_WIDTH)
    cq, ckv, kr, dq, dk, dv, hq, hff, hfb, hi, hg = _split_cols(w_in, in_sizes)
    zeros = lambda rows, n: jnp.zeros((depth, rows, n), F32)
    rope_tile = lambda w: jnp.concatenate([zeros(w.shape[1], MLA_NOPE), w, zeros(w.shape[1], LANES - MLA_NOPE - MLA_ROPE)], -1)
    dqs = dq * (DIFF_SCALE * LOG2_E)
    win = jnp.concatenate([cq, ckv, rope_tile(kr), rope_tile(_swap32(kr)), dqs, _swap32(dqs), dk, _swap32(dk), dv,
                           hq, hff, hfb, hi, hg], axis=-1).astype(BF16)
    per = MLA_NOPE + MLA_ROPE
    uq_tiles, uq_sw_tiles = [], []
    for h in range(MLA_HEADS):
        nope = w_uq[..., h * per:h * per + MLA_NOPE]
        rope = w_uq[..., h * per + MLA_NOPE:(h + 1) * per]
        uq_tiles.append(jnp.concatenate([nope, rope, zeros(MLA_Q_RANK, LANES - per)], -1))
        uq_sw_tiles.append(rope_tile(_swap32(rope)))
    wuq = (jnp.concatenate(uq_tiles + uq_sw_tiles, axis=-1) * (MLA_SCALE * LOG2_E)).astype(BF16)
    kv = MLA_NOPE + MLA_V
    k_tiles = [jnp.concatenate([w_ukv[..., h * kv:h * kv + MLA_NOPE], zeros(MLA_KV_RANK, LANES - MLA_NOPE)], -1)
               for h in range(MLA_HEADS)]
    v_cols = [w_ukv[..., h * kv + MLA_NOPE:(h + 1) * kv] for h in range(MLA_HEADS)]
    wukv = jnp.concatenate(k_tiles + v_cols, axis=-1).astype(BF16)
    return win, wuq, wukv


def _rope_tables(seq, ctx_len):
    t = jnp.arange(seq)
    row = (t // GRID_W).astype(F32)
    col = (t % GRID_W).astype(F32)
    n_freq = ROPE_DIM // 4
    freqs = ROPE_BASE ** (-jnp.arange(n_freq, dtype=F32) / n_freq)
    ang_r = row[:, None] * freqs
    ang_c = col[:, None] * freqs
    cr, sr, cc, sc = jnp.cos(ang_r), jnp.sin(ang_r), jnp.cos(ang_c), jnp.sin(ang_c)
    c32 = jnp.concatenate([cr, cr, cc, cc], axis=-1)
    s32 = jnp.concatenate([-sr, sr, -sc, sc], axis=-1)
    c32 = jnp.concatenate([jnp.ones((ctx_len, ROPE_DIM), F32), c32], axis=0)
    s32 = jnp.concatenate([jnp.zeros((ctx_len, ROPE_DIM), F32), s32], axis=0)
    n = c32.shape[0]
    cd = jnp.tile(c32, (1, LANES // ROPE_DIM))
    sd = jnp.tile(s32, (1, LANES // ROPE_DIM))
    pad = jnp.zeros((n, LANES - MLA_NOPE - MLA_ROPE), F32)
    cq = jnp.concatenate([jnp.ones((n, MLA_NOPE), F32), c32, pad], axis=-1)
    sq = jnp.concatenate([jnp.zeros((n, MLA_NOPE), F32), s32, pad], axis=-1)
    return cd, sd, cq, sq


def kernel(x, c, ctx, c_ctx, w_ada, b_ada, g_norm1, g_norm2, w_in, g_q_norm, w_uq, g_kv_norm, w_ukv,
           diff_lambda, g_diff_norm, hgrn_lower_bounds, g_hgrn_norm, w_out, w_ffn_gate, w_ffn_up,
           w_ffn_down, g_final):
    bsz, seq, d = x.shape
    ctx_len = ctx.shape[1]
    depth = w_in.shape[0]
    assert ctx_len % TOKEN_TILE == 0 and seq % TOKEN_TILE == 0 and seq % GRID_W == 0
    ctx_tiles = ctx_len // TOKEN_TILE
    ctx_scan_chunks = ctx_len // SCAN_CHUNK

    cvec = jnp.concatenate([c, c_ctx[None, :], jnp.zeros((8 - bsz - 1, d), F32)], axis=0)
    mods = _adaln(cvec, w_ada, b_ada)
    win, wuq, wukv = _layout_weights(w_in, w_uq, w_ukv)
    wo, wg, wu, wd = (w.astype(BF16) for w in (w_out, w_ffn_gate, w_ffn_up, w_ffn_down))
    tables = _rope_tables(seq, ctx_len)

    p = jax.nn.softmax(hgrn_lower_bounds.astype(F32), axis=0)
    cum = jnp.cumsum(p, axis=0)
    lb = (cum - cum[0:1]).reshape(depth, 1, 2 * HGRN_WIDTH)
    llb, l1mlb = jnp.log(lb), jnp.log1p(-lb)

    idx = np.arange(HGRN_WIDTH)
    seg = jnp.asarray(idx[:, None] // HGRN_DV == idx[None, :] // HGRN_DV, BF16)

    xall = jnp.concatenate([ctx, x], axis=1)
    for l in range(depth):
        lam_init = 0.8 - 0.6 * math.exp(-0.3 * l)
        mods3 = mods[l, 0:bsz + 1].reshape(bsz + 1, 1, 6 * d)
        qm, km, vtm, dq, dk, vtd, hq, hv, lf, hg = _pre_mixer(
            xall, mods3, ctx_tiles, g_norm1[l][None], win[l], g_q_norm[l][None], wuq[l], g_kv_norm[l][None], wukv[l],
            llb[l], l1mlb[l], tables)
        mla = _mla_attention(qm, km, vtm, ctx_tiles)
        g_col = jnp.tile(g_diff_norm[l], 2)[:, None]
        diff = _diff_attention(dq, dk, vtd, diff_lambda[l].astype(F32), g_col, ctx_tiles, lam_init)
        o_f = _hgrn_scan(hq, hv, lf, False, ctx_scan_chunks)
        o_b = _hgrn_scan(hq, hv, lf, True, ctx_scan_chunks)
        gh = jnp.tile(g_hgrn_norm[l], HGRN_HEADS)[None]
        xall = _post_mixer(xall, mods3, ctx_tiles, mla, diff, o_f, o_b, hg, gh, seg, wo[l], g_norm2[l][None],
                           wg[l], wu[l], wd[l])
    return _final_norm(xall, g_final[None], ctx_tiles, seq)
```

```python
import functools
import math

import jax
import jax.numpy as jnp
import numpy as np
from jax import lax
from jax.experimental import pallas as pl
from jax.experimental.pallas import tpu as pltpu

F32 = jnp.float32
BF16 = jnp.bfloat16

GRID_W = 64
EPS = 1e-6
ROPE_BASE = 10000.0
ROPE_DIM = 32

MLA_HEADS = 4
MLA_Q_RANK = 256
MLA_KV_RANK = 128
MLA_NOPE = 64
MLA_ROPE = ROPE_DIM
MLA_V = 64
MLA_SCALE = (MLA_NOPE + MLA_ROPE) ** -0.5

DIFF_HEADS = 4
DIFF_DK = ROPE_DIM
DIFF_DV = 2 * DIFF_DK
DIFF_SCALE = DIFF_DK ** -0.5
LOG2_E = math.log2(math.e)

HGRN_HEADS = 8
HGRN_DK = 64
HGRN_DV = 64
HGRN_WIDTH = HGRN_HEADS * HGRN_DV

LANES = 128
MXU_DIM = 256
VMEM_LIMIT_BYTES = 56 * 1024 * 1024

TOKEN_TILE = MXU_DIM
SCAN_CHUNK = 64

_IN_GROUPS = (
    ('cq', MLA_Q_RANK), ('ckv', MLA_KV_RANK), ('kr', LANES), ('kr_sw', LANES),
    ('dq', 256), ('dq_sw', 256), ('dk', 256), ('dk_sw', 256), ('dv', 256),
    ('hq', HGRN_WIDTH), ('hf', 2 * HGRN_WIDTH), ('hi', HGRN_WIDTH), ('hg', HGRN_WIDTH),
)
_IN_OFF = {}
_o = 0
for _n, _w in _IN_GROUPS:
    _IN_OFF[_n] = (_o, _o + _w)
    _o += _w
IN_WIDTH_PADDED = _o

_ROPE_SWAP = np.concatenate([np.arange(8, 16), np.arange(0, 8), np.arange(24, 32), np.arange(16, 24)])


def _silu(x):
    return x * jax.nn.sigmoid(x)


def _rms(x, g):
    return x * lax.rsqrt(jnp.mean(x * x, axis=-1, keepdims=True) + EPS) * g


def _dot(a, b):
    return jnp.dot(a, b, preferred_element_type=F32)


def _dot_nt(a, b):
    return lax.dot_general(a, b, (((1,), (1,)), ((), ())), preferred_element_type=F32)


def _params(n_grid):
    return pltpu.CompilerParams(dimension_semantics=('arbitrary',) * n_grid,
                                vmem_limit_bytes=VMEM_LIMIT_BYTES)


def _const_spec(shape):
    nd = len(shape)
    return pl.BlockSpec(shape, lambda *_: (0,) * nd, pipeline_mode=pl.Buffered(1))


def _adaln_kernel(n_rows, cvt_ref, w_ref, b_ref, o_ref):
    s = _silu(cvt_ref[...])
    w = w_ref[0]
    outs = [jnp.sum(s[:, r:r + 1] * w, axis=0, keepdims=True) + b_ref[0] for r in range(n_rows)]
    outs.append(jnp.zeros((o_ref.shape[1] - n_rows, w.shape[1]), F32))
    o_ref[0] = jnp.concatenate(outs, axis=0)


def _adaln(cvec_t, n_rows, w_ada, b_ada):
    depth, d, n = w_ada.shape
    tn = 1536
    return pl.pallas_call(
        functools.partial(_adaln_kernel, n_rows),
        grid=(depth, n // tn),
        in_specs=[pl.BlockSpec((d, 8), lambda l, j: (0, 0)),
                  pl.BlockSpec((1, d, tn), lambda l, j: (l, 0, j)),
                  pl.BlockSpec((1, 1, tn), lambda l, j: (l, 0, j))],
        out_specs=pl.BlockSpec((1, 8, tn), lambda l, j: (l, 0, j)),
        out_shape=jax.ShapeDtypeStruct((depth, 8, n), F32),
        compiler_params=_params(2),
        name='adaln',
    )(cvec_t, w_ada, b_ada.reshape(depth, 1, n))


def _pre_kernel(x_ref, mod_ref, g1_ref, win_ref, gq_ref, wuq_ref, gkv_ref, wukv_ref, llb_ref, l1mlb_ref,
                cd_ref, sd_ref, cq_ref, sq_ref,
                qm_ref, km_ref, vtm_ref, dq_ref, dk_ref, vtd_ref, hq_ref, hv_ref, lf_ref, hg_ref):
    d = x_ref.shape[-1]
    x = x_ref[0]
    mod = mod_ref[0]
    sh1, sc1 = mod[:, 0:d], mod[:, d:2 * d]
    hb = (_rms(x, g1_ref[...]) * (1 + sc1) + sh1).astype(BF16)

    def proj(name):
        lo, hi = _IN_OFF[name]
        return _dot(hb, win_ref[:, lo:hi])

    cq_t, sq_t = cq_ref[...], sq_ref[...]
    cqn = _rms(proj('cq'), gq_ref[...]).astype(BF16)
    uq = _dot(cqn, wuq_ref[...])
    for h in range(MLA_HEADS):
        a = uq[:, h * LANES:(h + 1) * LANES]
        b = uq[:, (MLA_HEADS + h) * LANES:(MLA_HEADS + h + 1) * LANES]
        qm_ref[0, h] = (a * cq_t + b * sq_t).astype(BF16)
    ckvn = _rms(proj('ckv'), gkv_ref[...]).astype(BF16)
    ukv = _dot(ckvn, wukv_ref[...])
    krt = proj('kr') * cq_t + proj('kr_sw') * sq_t
    for h in range(MLA_HEADS):
        km_ref[0, h] = (ukv[:, h * LANES:(h + 1) * LANES] + krt).astype(BF16)
    for p in range(MLA_HEADS // 2):
        lo = (MLA_HEADS + p) * LANES
        vtm_ref[0, p, 0] = ukv[:, lo:lo + LANES].T.astype(BF16)
    cd2 = jnp.concatenate([cd_ref[...]] * 2, axis=1)
    sd2 = jnp.concatenate([sd_ref[...]] * 2, axis=1)
    dq_ref[0] = (proj('dq') * cd2 + proj('dq_sw') * sd2).astype(BF16)
    dk_ref[0] = (proj('dk') * cd2 + proj('dk_sw') * sd2).astype(BF16)
    dv = proj('dv')
    for p in range(DIFF_HEADS // 2):
        vtd_ref[0, p, 0] = dv[:, p * LANES:(p + 1) * LANES].T.astype(BF16)
    hq_ref[0] = _silu(proj('hq'))
    z = proj('hf')
    ls = jnp.minimum(z, 0.0) - jnp.log1p(jnp.exp(-jnp.abs(z)))
    a = llb_ref[...]
    b = l1mlb_ref[...] + ls
    lf_ref[0] = jnp.maximum(a, b) + jnp.log1p(jnp.exp(-jnp.abs(a - b)))
    hv_ref[0] = proj('hi').astype(BF16)
    hg_ref[0] = proj('hg')


def _mod_spec(mods3, ctx_tiles):
    ctx_row = mods3.shape[0] - 1
    return pl.BlockSpec((1, 1, mods3.shape[-1]), lambda b, i: (jnp.where(i < ctx_tiles, ctx_row, b), 0, 0))


def _pre_mixer(xall, mods3, ctx_tiles, g1, win, gq, wuq, gkv, wukv, llb, l1mlb, tables):
    bsz, s, d = xall.shape
    tm = TOKEN_TILE
    ns = s // tm
    grid = (bsz, ns)
    tok = lambda w: pl.BlockSpec((1, tm, w), lambda b, i: (b, i, 0))
    tab = pl.BlockSpec((tm, LANES), lambda b, i: (i, 0))
    head4 = pl.BlockSpec((1, MLA_HEADS, tm, LANES), lambda b, i: (b, 0, i, 0))
    vt = pl.BlockSpec((1, 2, 1, LANES, tm), lambda b, i: (b, 0, i, 0, 0))
    in_specs = [
        tok(d),
        _mod_spec(mods3, ctx_tiles),
        _const_spec(g1.shape), _const_spec(win.shape), _const_spec(gq.shape), _const_spec(wuq.shape),
        _const_spec(gkv.shape), _const_spec(wukv.shape), _const_spec(llb.shape), _const_spec(l1mlb.shape),
        tab, tab, tab, tab,
    ]
    sds = jax.ShapeDtypeStruct
    out_shape = (
        sds((bsz, MLA_HEADS, s, LANES), BF16), sds((bsz, MLA_HEADS, s, LANES), BF16),
        sds((bsz, 2, ns, LANES, tm), BF16),
        sds((bsz, s, 256), BF16), sds((bsz, s, 256), BF16), sds((bsz, 2, ns, LANES, tm), BF16),
        sds((bsz, s, HGRN_WIDTH), F32), sds((bsz, s, HGRN_WIDTH), BF16),
        sds((bsz, s, 2 * HGRN_WIDTH), F32), sds((bsz, s, HGRN_WIDTH), F32),
    )
    out_specs = (head4, head4, vt, tok(256), tok(256), vt,
                 tok(HGRN_WIDTH), tok(HGRN_WIDTH), tok(2 * HGRN_WIDTH), tok(HGRN_WIDTH))
    return pl.pallas_call(
        _pre_kernel, grid=grid, in_specs=in_specs, out_specs=out_specs, out_shape=out_shape,
        compiler_params=_params(2), name='pre_mixer',
    )(xall, mods3, g1, win, gq, wuq, gkv, wukv, llb, l1mlb, *tables)


def _attend(streams, n_groups, tkg, s_ref):
    tq = s_ref.shape[-1]

    def score(g, slot):
        for si, (q, key_rows, _) in enumerate(streams):
            s_ref[si, slot, 0:tkg, :] = _dot_nt(key_rows(g), q)

    def softmax_pv(g, slot, carry):
        new = []
        for si, ((_, _, value_t), (m, l, acc)) in enumerate(zip(streams, carry)):
            s = s_ref[si, slot, 0:tkg, :]
            m_new = jnp.maximum(m, jnp.max(s, axis=0, keepdims=True))
            alpha = jnp.exp2(m - m_new)
            p = jnp.exp2(s - m_new)
            l = alpha * l + jnp.sum(p, axis=0, keepdims=True)
            acc = alpha * acc + _dot(value_t(g), p.astype(BF16))
            new.append((m_new, l, acc))
        return tuple(new)

    def body(t, carry):
        g = 2 * t
        score(g + 1, 1)
        carry = softmax_pv(g, 0, carry)
        score(g + 2, 0)
        return softmax_pv(g + 1, 1, carry)

    carry = tuple((jnp.full((1, tq), -jnp.inf, F32), jnp.zeros((1, tq), F32), jnp.zeros((LANES, tq), F32))
                  for _ in streams)
    score(0, 0)
    n_pairs = (n_groups - 1) // 2
    if n_pairs > 0:
        carry = lax.fori_loop(0, n_pairs, body, carry)
    g = 2 * n_pairs
    if n_groups % 2 == 0:
        score(g + 1, 1)
        carry = softmax_pv(g, 0, carry)
        carry = softmax_pv(g + 1, 1, carry)
    else:
        carry = softmax_pv(g, 0, carry)
    return [(acc, l) for (_, l, acc) in carry]


def _chunks_per_group(n_chunks):
    return next(u for u in (3, 4, 2, 1) if n_chunks % u == 0)


def _for_context_and_latent(i, ctx_chunks, all_chunks, run):
    @pl.when(i < ctx_chunks)
    def _():
        u = _chunks_per_group(ctx_chunks)
        run(ctx_chunks // u, u)

    @pl.when(i >= ctx_chunks)
    def _():
        u = _chunks_per_group(all_chunks)
        run(all_chunks // u, u)


def _key_rows(k_ref, lead, tk, u):
    return lambda g: k_ref[lead + (pl.ds(pl.multiple_of(g * (u * tk), u * tk), u * tk), slice(None))]


def _value_t(vt_ref, u):
    def get(g):
        parts = [vt_ref[0, 0, g * u + c] for c in range(u)]
        return parts[0] if u == 1 else jnp.concatenate(parts, axis=1)
    return get


def _score_scratch(n_streams, ctx_chunks, all_chunks, tk, tq):
    rows = max(_chunks_per_group(ctx_chunks), _chunks_per_group(all_chunks)) * tk
    return pltpu.VMEM((n_streams, 2, rows, tq), F32)


def _mla_kernel(ctx_chunks, q_ref, k_ref, vt_ref, o_ref, s_ref):
    tq = q_ref.shape[2]
    tk = vt_ref.shape[-1]

    def run(n_groups, u):
        streams = [(q_ref[0, hh], _key_rows(k_ref, (0, hh), tk, u), _value_t(vt_ref, u)) for hh in range(2)]
        (acc0, l0), (acc1, l1) = _attend(streams, n_groups, u * tk, s_ref)
        row = lax.broadcasted_iota(jnp.int32, (LANES, tq), 0)
        ot = jnp.where(row < MLA_V, acc0 / l0, acc1 / l1)
        o_ref[0] = ot.T.astype(o_ref.dtype)

    _for_context_and_latent(pl.program_id(2), ctx_chunks, vt_ref.shape[2], run)


def _mla_attention(qm, km, vtm, ctx_chunks):
    bsz, _, s, _ = qm.shape
    nc, tk = vtm.shape[2], vtm.shape[4]
    tq = TOKEN_TILE
    return pl.pallas_call(
        functools.partial(_mla_kernel, ctx_chunks),
        grid=(bsz, MLA_HEADS // 2, s // tq),
        in_specs=[pl.BlockSpec((1, 2, tq, LANES), lambda b, p, i: (b, p, i, 0)),
                  pl.BlockSpec((1, 2, s, LANES), lambda b, p, i: (b, p, 0, 0)),
                  pl.BlockSpec((1, 1, nc, LANES, tk), lambda b, p, i: (b, p, 0, 0, 0))],
        out_specs=pl.BlockSpec((1, tq, LANES), lambda b, p, i: (b, i, p)),
        out_shape=jax.ShapeDtypeStruct((bsz, s, MLA_HEADS * MLA_V), BF16),
        scratch_shapes=[_score_scratch(2, ctx_chunks, nc, tk, tq)],
        compiler_params=_params(3), name='mla_attention',
    )(qm, km, vtm)


def _diff_kernel(ctx_chunks, lam_init, q_ref, k_ref, vt_ref, lam_ref, g_ref, o_ref, s_ref):
    tq = q_ref.shape[1]
    tk = vt_ref.shape[-1]
    pair = pl.program_id(1)

    def run(n_groups, u):
        lp = lam_ref[...]
        lam = (jnp.exp(jnp.sum(lp[0:1] * lp[1:2], axis=1, keepdims=True))
               - jnp.exp(jnp.sum(lp[2:3] * lp[3:4], axis=1, keepdims=True)) + lam_init)
        q = q_ref[0]
        lane = lax.broadcasted_iota(jnp.int32, q.shape, 1)
        key_rows, value_t = _key_rows(k_ref, (0,), tk, u), _value_t(vt_ref, u)
        streams = []
        for hh in range(2):
            for mp in range(2):
                base = ((2 * pair + hh) * 2 + mp) * DIFF_DK
                qsel = jnp.where((lane >= base) & (lane < base + DIFF_DK), q, jnp.zeros_like(q))
                streams.append((qsel, key_rows, value_t))
        res = _attend(streams, n_groups, u * tk, s_ref)
        outs = []
        for hh in range(2):
            (a1, l1), (a2, l2) = res[2 * hh], res[2 * hh + 1]
            outs.append(a1 / l1 - lam * (a2 / l2))
        row = lax.broadcasted_iota(jnp.int32, (LANES, tq), 0)
        first = row < DIFF_DV
        ot = jnp.where(first, outs[0], outs[1])
        sq = ot * ot
        ms = jnp.where(first,
                       jnp.sum(sq[0:DIFF_DV], axis=0, keepdims=True),
                       jnp.sum(sq[DIFF_DV:], axis=0, keepdims=True)) * (1.0 / DIFF_DV)
        y = ot * lax.rsqrt(ms + EPS) * g_ref[...] * (1 - lam_init)
        o_ref[0] = y.T.astype(o_ref.dtype)

    _for_context_and_latent(pl.program_id(2), ctx_chunks, vt_ref.shape[2], run)


def _diff_attention(dq, dk, vtd, lam_params, g_col, ctx_chunks, lam_init):
    bsz, s, w = dq.shape
    nc, tk = vtd.shape[2], vtd.shape[4]
    tq = TOKEN_TILE
    return pl.pallas_call(
        functools.partial(_diff_kernel, ctx_chunks, lam_init),
        grid=(bsz, DIFF_HEADS // 2, s // tq),
        in_specs=[pl.BlockSpec((1, tq, w), lambda b, p, i: (b, i, 0)),
                  pl.BlockSpec((1, s, w), lambda b, p, i: (b, 0, 0)),
                  pl.BlockSpec((1, 1, nc, LANES, tk), lambda b, p, i: (b, p, 0, 0, 0)),
                  pl.BlockSpec(lam_params.shape, lambda b, p, i: (0, 0)),
                  pl.BlockSpec(g_col.shape, lambda b, p, i: (0, 0))],
        out_specs=pl.BlockSpec((1, tq, LANES), lambda b, p, i: (b, i, p)),
        out_shape=jax.ShapeDtypeStruct((bsz, s, DIFF_HEADS * DIFF_DV), BF16),
        scratch_shapes=[_score_scratch(4, ctx_chunks, nc, tk, tq)],
        compiler_params=_params(3), name='diff_attention',
    )(dq, dk, vtd, lam_params, g_col)


def _hgrn_kernel(reverse, q_ref, v_ref, lf_ref, o_ref, st_ref):
    c = SCAN_CHUNK
    tile, width = q_ref.shape[1], q_ref.shape[2]
    n_ch, pairs = tile // c, width // LANES
    log_c = c.bit_length() - 1
    tok = lax.broadcasted_iota(jnp.int32, (c, 2 * c), 0)
    src = lax.broadcasted_iota(jnp.int32, (c, 2 * c), 1) & (c - 1)
    pos_t = (c - 1 - tok) if reverse else tok
    pos_s = (c - 1 - src) if reverse else src
    tok_col = lax.broadcasted_iota(jnp.int32, (tile, 1), 0) & (c - 1)
    pos_col = (c - 1 - tok_col) if reverse else tok_col
    head_a_all = (lax.broadcasted_iota(jnp.int32, (tile, width), 1) & (LANES - 1)) < HGRN_DK
    r2 = lax.broadcasted_iota(jnp.int32, (LANES, LANES), 0)
    c2 = lax.broadcasted_iota(jnp.int32, (LANES, LANES), 1)
    same_head = (r2 < HGRN_DV) == (c2 < HGRN_DK)
    tok2 = lax.broadcasted_iota(jnp.int32, (tile, tile), 0)
    src2 = lax.broadcasted_iota(jnp.int32, (tile, tile), 1)
    tri = (((tok2 >> log_c) == (src2 >> log_c)) & ((src2 >= tok2) if reverse else (src2 <= tok2))).astype(BF16)
    earlier = lambda x, n: pltpu.roll(x, (tile - n) if reverse else n, 0)
    later_by = lambda x, n: pltpu.roll(x, n if reverse else (tile - n), 0)

    @pl.when(pl.program_id(1) == 0)
    def _():
        st_ref[...] = jnp.zeros_like(st_ref)

    rows = lambda ci: slice(ci * c, (ci + 1) * c)
    lanes = lambda p: slice(p * LANES, (p + 1) * LANES)
    cells = [(ci, p) for ci in range(n_ch) for p in range(pairs)]

    q, v, lf = q_ref[0], v_ref[0], lf_ref[0]
    p0 = lf.astype(BF16)
    r0 = lf - p0.astype(F32)
    p1 = r0.astype(BF16)
    p2 = (r0 - p1.astype(F32)).astype(BF16)
    cum = (_dot(tri, p0) + _dot(tri, p1) + _dot(tri, p2)) * LOG2_E
    k = 1.0 - jnp.exp(lf)
    zero = jnp.zeros((), BF16)
    q16 = q.astype(BF16)
    k16 = k.astype(BF16)
    ka, kb = jnp.where(head_a_all, k16, zero), jnp.where(head_a_all, zero, k16)
    va, vb = jnp.where(head_a_all, v, zero), jnp.where(head_a_all, zero, v)

    scores = {cell: None for cell in cells}

    def add_stage(qd, kda, kdb, keep):
        for ci, p in cells:
            kd2 = jnp.concatenate([kda[rows(ci), lanes(p)], kdb[rows(ci), lanes(p)]], axis=0)
            sc = jnp.where(keep, _dot_nt(qd[rows(ci), lanes(p)], kd2), 0.0)
            scores[ci, p] = sc if scores[ci, p] is None else scores[ci, p] + sc

    end = cum
    block = 2
    while block <= c:
        half = block // 2
        first = (pos_col & (block - 1)) < half
        ref = jnp.where(first, end, earlier(end, half))
        end = jnp.where(first, later_by(end, half), end)
        e = jnp.exp2(-jnp.abs(cum - ref)).astype(BF16)
        e_q = jnp.where(first, zero, e)
        e_k = jnp.where(first, e, zero)
        shift = block.bit_length() - 1
        add_stage(q16 * e_q, ka * e_k, kb * e_k, (pos_t >> shift) == (pos_s >> shift))
        block *= 2
    add_stage(q16, ka, kb, tok == src)
    tot = end
    qs = (q * jnp.exp2(cum)).astype(BF16)
    ks = (k * jnp.exp2(tot - cum)).astype(BF16)

    intra, upd = {}, {}
    for ci, p in cells:
        v2 = jnp.concatenate([va[rows(ci), lanes(p)], vb[rows(ci), lanes(p)]], axis=0)
        intra[ci, p] = _dot(scores[ci, p].astype(BF16), v2)
        vt = v[rows(ci), lanes(p)].astype(F32).T.astype(BF16)
        upd[ci, p] = jnp.where(same_head, _dot(vt, ks[rows(ci), lanes(p)]), 0.0)

    states = [st_ref[p] for p in range(pairs)]
    for ci in (range(n_ch - 1, -1, -1) if reverse else range(n_ch)):
        decay = jnp.exp2(tot[ci * c:ci * c + 1, :])
        outs = []
        for p in range(pairs):
            outs.append(_dot_nt(qs[rows(ci), lanes(p)], states[p].astype(BF16)) + intra[ci, p])
            states[p] = states[p] * decay[:, lanes(p)] + upd[ci, p]
        o_ref[0, rows(ci), :] = jnp.concatenate(outs, axis=1)
    for p in range(pairs):
        st_ref[p] = states[p]


def _hgrn_scan(hq, hv, lf, reverse, ctx_tiles):
    bsz, s, w = hq.shape
    tile = TOKEN_TILE
    ns = s // tile
    if reverse:
        blk = lambda i: jnp.where(i < ctx_tiles, ctx_tiles - 1 - i, ns - 1 + ctx_tiles - i)
    else:
        blk = lambda i: i
    spec = lambda lane_block: pl.BlockSpec((1, tile, w), lambda b, i: (b, blk(i), lane_block))
    return pl.pallas_call(
        functools.partial(_hgrn_kernel, reverse),
        grid=(bsz, ns),
        in_specs=[spec(0), spec(0), spec(1 if reverse else 0)],
        out_specs=spec(0),
        out_shape=jax.ShapeDtypeStruct((bsz, s, w), F32),
        scratch_shapes=[pltpu.VMEM((w // LANES, LANES, LANES), F32)],
        compiler_params=_params(2), name='hgrn_bwd' if reverse else 'hgrn_fwd',
    )(hq, hv, lf)


def _post_kernel(ffn_chunk, x_ref, mod_ref, mla_ref, diff_ref, of_ref, ob_ref, hg_ref, gh_ref, seg_ref,
                 wo_ref, g2_ref, wg_ref, wu_ref, wd_ref, o_ref):
    d = x_ref.shape[-1]
    x = x_ref[0]
    mod = mod_ref[0]
    gt1, sh2, sc2, gt2 = (mod[:, 2 * d:3 * d], mod[:, 3 * d:4 * d], mod[:, 4 * d:5 * d], mod[:, 5 * d:6 * d])
    o = of_ref[0] + ob_ref[0]
    sq = o * o
    sq_hi = sq.astype(BF16)
    sq_lo = (sq - sq_hi.astype(F32)).astype(BF16)
    seg = seg_ref[...]
    ms = (_dot(sq_hi, seg) + _dot(sq_lo, seg)) * (1.0 / HGRN_DV)
    hn = (o * lax.rsqrt(ms + EPS) * gh_ref[...] * _silu(hg_ref[0])).astype(BF16)
    na, nb = mla_ref.shape[-1], diff_ref.shape[-1]
    y = (_dot(mla_ref[0], wo_ref[0:na, :]) + _dot(diff_ref[0], wo_ref[na:na + nb, :])
         + _dot(hn, wo_ref[na + nb:, :]))
    x1 = x + gt1 * y
    h2 = (_rms(x1, g2_ref[...]) * (1 + sc2) + sh2).astype(BF16)
    dff = wg_ref.shape[1]
    acc = jnp.zeros_like(x1)
    for lo in range(0, dff, ffn_chunk):
        g = _dot(h2, wg_ref[:, lo:lo + ffn_chunk])
        u = _dot(h2, wu_ref[:, lo:lo + ffn_chunk])
        acc = acc + _dot((_silu(g) * u).astype(BF16), wd_ref[lo:lo + ffn_chunk, :])
    o_ref[0] = x1 + gt2 * acc


def _post_mixer(xall, mods3, ctx_tiles, mla, diff, o_f, o_b, hg, gh, seg, wo, g2, wg, wu, wd):
    bsz, s, d = xall.shape
    tm = TOKEN_TILE
    dff = wg.shape[1]
    ffn_chunk = dff // 2
    tok = lambda w: pl.BlockSpec((1, tm, w), lambda b, i: (b, i, 0))
    in_specs = [
        tok(d),
        _mod_spec(mods3, ctx_tiles),
        tok(mla.shape[-1]), tok(diff.shape[-1]), tok(HGRN_WIDTH), tok(HGRN_WIDTH), tok(HGRN_WIDTH),
        _const_spec(gh.shape), _const_spec(seg.shape), _const_spec(wo.shape), _const_spec(g2.shape),
        _const_spec(wg.shape), _const_spec(wu.shape), _const_spec(wd.shape),
    ]
    return pl.pallas_call(
        functools.partial(_post_kernel, ffn_chunk),
        grid=(bsz, s // tm), in_specs=in_specs, out_specs=tok(d),
        out_shape=jax.ShapeDtypeStruct((bsz, s, d), F32),
        compiler_params=_params(2), name='post_mixer',
    )(xall, mods3, mla, diff, o_f, o_b, hg, gh, seg, wo, g2, wg, wu, wd)


def _final_kernel(x_ref, g_ref, o_ref):
    o_ref[0] = _rms(x_ref[0], g_ref[...])


def _final_norm(xall, g, ctx_tiles, seq):
    bsz, _, d = xall.shape
    tm = TOKEN_TILE
    return pl.pallas_call(
        _final_kernel,
        grid=(bsz, seq // tm),
        in_specs=[pl.BlockSpec((1, tm, d), lambda b, i: (b, i + ctx_tiles, 0)),
                  pl.BlockSpec(g.shape, lambda b, i: (0, 0))],
        out_specs=pl.BlockSpec((1, tm, d), lambda b, i: (b, i, 0)),
        out_shape=jax.ShapeDtypeStruct((bsz, seq, d), F32),
        compiler_params=_params(2), name='final_norm',
    )(xall, g)


def _split_cols(w, sizes):
    return jnp.split(w, np.cumsum(sizes)[:-1].tolist(), axis=-1)


def _swap32(w):
    lead = w.shape[:-1]
    return w.reshape(*lead, -1, ROPE_DIM)[..., _ROPE_SWAP].reshape(*lead, -1)


def _layout_weights(w_in, w_uq, w_ukv):
    depth, d, _ = w_in.shape
    in_sizes = (MLA_Q_RANK, MLA_KV_RANK, MLA_ROPE, 256, 256, 256,
                HGRN_WIDTH, HGRN_WIDTH, HGRN_WIDTH, HGRN_WIDTH, HGRN_WIDTH)
    cq, ckv, kr, dq, dk, dv, hq, hff, hfb, hi, hg = _split_cols(w_in, in_sizes)
    zeros = lambda rows, n: jnp.zeros((depth, rows, n), F32)
    rope_tile = lambda w: jnp.concatenate([zeros(w.shape[1], MLA_NOPE), w, zeros(w.shape[1], LANES - MLA_NOPE - MLA_ROPE)], -1)
    dqs = dq * (DIFF_SCALE * LOG2_E)
    win = jnp.concatenate([cq, ckv, rope_tile(kr), rope_tile(_swap32(kr)), dqs, _swap32(dqs), dk, _swap32(dk), dv,
                           hq, hff, hfb, hi, hg], axis=-1).astype(BF16)
    per = MLA_NOPE + MLA_ROPE
    uq_tiles, uq_sw_tiles = [], []
    for h in range(MLA_HEADS):
        nope = w_uq[..., h * per:h * per + MLA_NOPE]
        rope = w_uq[..., h * per + MLA_NOPE:(h + 1) * per]
        uq_tiles.append(jnp.concatenate([nope, rope, zeros(MLA_Q_RANK, LANES - per)], -1))
        uq_sw_tiles.append(rope_tile(_swap32(rope)))
    wuq = (jnp.concatenate(uq_tiles + uq_sw_tiles, axis=-1) * (MLA_SCALE * LOG2_E)).astype(BF16)
    kv = MLA_NOPE + MLA_V
    k_tiles = [jnp.concatenate([w_ukv[..., h * kv:h * kv + MLA_NOPE], zeros(MLA_KV_RANK, LANES - MLA_NOPE)], -1)
               for h in range(MLA_HEADS)]
    v_cols = [w_ukv[..., h * kv + MLA_NOPE:(h + 1) * kv] for h in range(MLA_HEADS)]
    wukv = jnp.concatenate(k_tiles + v_cols, axis=-1).astype(BF16)
    return win, wuq, wukv


def _rope_tables(seq, ctx_len):
    t = jnp.arange(seq)
    row = (t // GRID_W).astype(F32)
    col = (t % GRID_W).astype(F32)
    n_freq = ROPE_DIM // 4
    freqs = ROPE_BASE ** (-jnp.arange(n_freq, dtype=F32) / n_freq)
    ang_r = row[:, None] * freqs
    ang_c = col[:, None] * freqs
    cr, sr, cc, sc = jnp.cos(ang_r), jnp.sin(ang_r), jnp.cos(ang_c), jnp.sin(ang_c)
    c32 = jnp.concatenate([cr, cr, cc, cc], axis=-1)
    s32 = jnp.concatenate([-sr, sr, -sc, sc], axis=-1)
    c32 = jnp.concatenate([jnp.ones((ctx_len, ROPE_DIM), F32), c32], axis=0)
    s32 = jnp.concatenate([jnp.zeros((ctx_len, ROPE_DIM), F32), s32], axis=0)
    n = c32.shape[0]
    cd = jnp.tile(c32, (1, LANES // ROPE_DIM))
    sd = jnp.tile(s32, (1, LANES // ROPE_DIM))
    pad = jnp.zeros((n, LANES - MLA_NOPE - MLA_ROPE), F32)
    cq = jnp.concatenate([jnp.ones((n, MLA_NOPE), F32), c32, pad], axis=-1)
    sq = jnp.concatenate([jnp.zeros((n, MLA_NOPE), F32), s32, pad], axis=-1)
    return cd, sd, cq, sq


def kernel(x, c, ctx, c_ctx, w_ada, b_ada, g_norm1, g_norm2, w_in, g_q_norm, w_uq, g_kv_norm, w_ukv,
           diff_lambda, g_diff_norm, hgrn_lower_bounds, g_hgrn_norm, w_out, w_ffn_gate, w_ffn_up,
           w_ffn_down, g_final):
    bsz, seq, d = x.shape
    ctx_len = ctx.shape[1]
    depth = w_in.shape[0]
    assert ctx_len % TOKEN_TILE == 0 and seq % TOKEN_TILE == 0 and seq % GRID_W == 0
    ctx_tiles = ctx_len // TOKEN_TILE

    cvec = jnp.concatenate([c, c_ctx[None, :], jnp.zeros((8 - bsz - 1, d), F32)], axis=0)
    mods = _adaln(cvec.T, bsz + 1, w_ada, b_ada)
    win, wuq, wukv = _layout_weights(w_in, w_uq, w_ukv)
    wo, wg, wu, wd = (w.astype(BF16) for w in (w_out, w_ffn_gate, w_ffn_up, w_ffn_down))
    tables = _rope_tables(seq, ctx_len)

    p = jax.nn.softmax(hgrn_lower_bounds.astype(F32), axis=0)
    cum = jnp.cumsum(p, axis=0)
    lb = (cum - cum[0:1]).reshape(depth, 1, 2 * HGRN_WIDTH)
    llb, l1mlb = jnp.log(lb), jnp.log1p(-lb)

    idx = np.arange(HGRN_WIDTH)
    seg = jnp.asarray(idx[:, None] // HGRN_DV == idx[None, :] // HGRN_DV, BF16)

    xall = jnp.concatenate([ctx, x], axis=1)
    for l in range(depth):
        lam_init = 0.8 - 0.6 * math.exp(-0.3 * l)
        mods3 = mods[l, 0:bsz + 1].reshape(bsz + 1, 1, 6 * d)
        qm, km, vtm, dq, dk, vtd, hq, hv, lf, hg = _pre_mixer(
            xall, mods3, ctx_tiles, g_norm1[l][None], win[l], g_q_norm[l][None], wuq[l], g_kv_norm[l][None], wukv[l],
            llb[l], l1mlb[l], tables)
        mla = _mla_attention(qm, km, vtm, ctx_tiles)
        g_col = jnp.tile(g_diff_norm[l], 2)[:, None]
        diff = _diff_attention(dq, dk, vtd, diff_lambda[l].astype(F32), g_col, ctx_tiles, lam_init)
        o_f = _hgrn_scan(hq, hv, lf, False, ctx_tiles)
        o_b = _hgrn_scan(hq, hv, lf, True, ctx_tiles)
        gh = jnp.tile(g_hgrn_norm[l], HGRN_HEADS)[None]
        xall = _post_mixer(xall, mods3, ctx_tiles, mla, diff, o_f, o_b, hg, gh, seg, wo[l], g_norm2[l][None],
                           wg[l], wu[l], wd[l])
    return _final_norm(xall, g_final[None], ctx_tiles, seq)
```

```python
import functools
import math

import jax
import jax.numpy as jnp
import numpy as np
from jax import lax
from jax.experimental import pallas as pl
from jax.experimental.pallas import tpu as pltpu

F32 = jnp.float32
BF16 = jnp.bfloat16

GRID_W = 64
EPS = 1e-6
ROPE_BASE = 10000.0
ROPE_DIM = 32

MLA_HEADS = 4
MLA_Q_RANK = 256
MLA_KV_RANK = 128
MLA_NOPE = 64
MLA_ROPE = ROPE_DIM
MLA_V = 64
MLA_SCALE = (MLA_NOPE + MLA_ROPE) ** -0.5

DIFF_HEADS = 4
DIFF_DK = ROPE_DIM
DIFF_DV = 2 * DIFF_DK
DIFF_SCALE = DIFF_DK ** -0.5
LOG2_E = math.log2(math.e)

HGRN_HEADS = 8
HGRN_DK = 64
HGRN_DV = 64
HGRN_WIDTH = HGRN_HEADS * HGRN_DV

LANES = 128
MXU_DIM = 256
VMEM_LIMIT_BYTES = 56 * 1024 * 1024

TOKEN_TILE = MXU_DIM
SCAN_CHUNK = 64

_IN_GROUPS = (
    ('cq', MLA_Q_RANK), ('ckv', MLA_KV_RANK), ('kr', LANES), ('kr_sw', LANES),
    ('dq', 256), ('dq_sw', 256), ('dk', 256), ('dk_sw', 256), ('dv', 256),
    ('hq', HGRN_WIDTH), ('hf', 2 * HGRN_WIDTH), ('hi', HGRN_WIDTH), ('hg', HGRN_WIDTH),
)
_IN_OFF = {}
_o = 0
for _n, _w in _IN_GROUPS:
    _IN_OFF[_n] = (_o, _o + _w)
    _o += _w
IN_WIDTH_PADDED = _o

_ROPE_SWAP = np.concatenate([np.arange(8, 16), np.arange(0, 8), np.arange(24, 32), np.arange(16, 24)])


def _silu(x):
    return x * jax.nn.sigmoid(x)


def _rms(x, g):
    return x * lax.rsqrt(jnp.mean(x * x, axis=-1, keepdims=True) + EPS) * g


def _dot(a, b):
    return jnp.dot(a, b, preferred_element_type=F32)


def _dot_nt(a, b):
    return lax.dot_general(a, b, (((1,), (1,)), ((), ())), preferred_element_type=F32)


def _params(n_grid):
    return pltpu.CompilerParams(dimension_semantics=('arbitrary',) * n_grid,
                                vmem_limit_bytes=VMEM_LIMIT_BYTES)


def _const_spec(shape):
    nd = len(shape)
    return pl.BlockSpec(shape, lambda *_: (0,) * nd, pipeline_mode=pl.Buffered(1))


def _adaln_kernel(n_rows, cvt_ref, w_ref, b_ref, o_ref):
    s = _silu(cvt_ref[...])
    w = w_ref[0]
    outs = [jnp.sum(s[:, r:r + 1] * w, axis=0, keepdims=True) + b_ref[0] for r in range(n_rows)]
    outs.append(jnp.zeros((o_ref.shape[1] - n_rows, w.shape[1]), F32))
    o_ref[0] = jnp.concatenate(outs, axis=0)


def _adaln(cvec_t, n_rows, w_ada, b_ada):
    depth, d, n = w_ada.shape
    tn = 1536
    return pl.pallas_call(
        functools.partial(_adaln_kernel, n_rows),
        grid=(depth, n // tn),
        in_specs=[pl.BlockSpec((d, 8), lambda l, j: (0, 0)),
                  pl.BlockSpec((1, d, tn), lambda l, j: (l, 0, j)),
                  pl.BlockSpec((1, 1, tn), lambda l, j: (l, 0, j))],
        out_specs=pl.BlockSpec((1, 8, tn), lambda l, j: (l, 0, j)),
        out_shape=jax.ShapeDtypeStruct((depth, 8, n), F32),
        compiler_params=_params(2),
        name='adaln',
    )(cvec_t, w_ada, b_ada.reshape(depth, 1, n))


def _pre_kernel(x_ref, mod_ref, g1_ref, win_ref, gq_ref, wuq_ref, gkv_ref, wukv_ref, llb_ref, l1mlb_ref,
                cd_ref, sd_ref, cq_ref, sq_ref,
                qm_ref, km_ref, vtm_ref, dq_ref, dk_ref, vtd_ref, hq_ref, hv_ref, lf_ref, hg_ref):
    d = x_ref.shape[-1]
    x = x_ref[0]
    mod = mod_ref[0]
    sh1, sc1 = mod[:, 0:d], mod[:, d:2 * d]
    hb = (_rms(x, g1_ref[...]) * (1 + sc1) + sh1).astype(BF16)

    proj_all = _dot(hb, win_ref[...])

    def proj(name):
        lo, hi = _IN_OFF[name]
        return proj_all[:, lo:hi]

    cq_t, sq_t = cq_ref[...], sq_ref[...]
    cqn = _rms(proj('cq'), gq_ref[...]).astype(BF16)
    uq = _dot(cqn, wuq_ref[...])
    for h in range(MLA_HEADS):
        a = uq[:, h * LANES:(h + 1) * LANES]
        b = uq[:, (MLA_HEADS + h) * LANES:(MLA_HEADS + h + 1) * LANES]
        qm_ref[0, h] = (a * cq_t + b * sq_t).astype(BF16)
    ckvn = _rms(proj('ckv'), gkv_ref[...]).astype(BF16)
    ukv = _dot(ckvn, wukv_ref[...])
    krt = proj('kr') * cq_t + proj('kr_sw') * sq_t
    for h in range(MLA_HEADS):
        km_ref[0, h] = (ukv[:, h * LANES:(h + 1) * LANES] + krt).astype(BF16)
    for p in range(MLA_HEADS // 2):
        lo = (MLA_HEADS + p) * LANES
        vtm_ref[0, p, 0] = ukv[:, lo:lo + LANES].T.astype(BF16)
    cd2 = jnp.concatenate([cd_ref[...]] * 2, axis=1)
    sd2 = jnp.concatenate([sd_ref[...]] * 2, axis=1)
    dq_ref[0] = (proj('dq') * cd2 + proj('dq_sw') * sd2).astype(BF16)
    dk_ref[0] = (proj('dk') * cd2 + proj('dk_sw') * sd2).astype(BF16)
    dv = proj('dv')
    for p in range(DIFF_HEADS // 2):
        vtd_ref[0, p, 0] = dv[:, p * LANES:(p + 1) * LANES].T.astype(BF16)
    hq_ref[0] = _silu(proj('hq'))
    z = proj('hf')
    ls = jnp.minimum(z, 0.0) - jnp.log1p(jnp.exp(-jnp.abs(z)))
    a = llb_ref[...]
    b = l1mlb_ref[...] + ls
    lf_ref[0] = jnp.maximum(a, b) + jnp.log1p(jnp.exp(-jnp.abs(a - b)))
    hv_ref[0] = proj('hi').astype(BF16)
    hg_ref[0] = proj('hg')


def _mod_spec(mods3, ctx_tiles):
    ctx_row = mods3.shape[0] - 1
    return pl.BlockSpec((1, 1, mods3.shape[-1]), lambda b, i: (jnp.where(i < ctx_tiles, ctx_row, b), 0, 0))


def _pre_mixer(xall, mods3, ctx_tiles, g1, win, gq, wuq, gkv, wukv, llb, l1mlb, tables):
    bsz, s, d = xall.shape
    tm = TOKEN_TILE
    ns = s // tm
    grid = (bsz, ns)
    tok = lambda w: pl.BlockSpec((1, tm, w), lambda b, i: (b, i, 0))
    tab = pl.BlockSpec((tm, LANES), lambda b, i: (i, 0))
    head4 = pl.BlockSpec((1, MLA_HEADS, tm, LANES), lambda b, i: (b, 0, i, 0))
    vt = pl.BlockSpec((1, 2, 1, LANES, tm), lambda b, i: (b, 0, i, 0, 0))
    in_specs = [
        tok(d),
        _mod_spec(mods3, ctx_tiles),
        _const_spec(g1.shape), _const_spec(win.shape), _const_spec(gq.shape), _const_spec(wuq.shape),
        _const_spec(gkv.shape), _const_spec(wukv.shape), _const_spec(llb.shape), _const_spec(l1mlb.shape),
        tab, tab, tab, tab,
    ]
    sds = jax.ShapeDtypeStruct
    out_shape = (
        sds((bsz, MLA_HEADS, s, LANES), BF16), sds((bsz, MLA_HEADS, s, LANES), BF16),
        sds((bsz, 2, ns, LANES, tm), BF16),
        sds((bsz, s, 256), BF16), sds((bsz, s, 256), BF16), sds((bsz, 2, ns, LANES, tm), BF16),
        sds((bsz, s, HGRN_WIDTH), F32), sds((bsz, s, HGRN_WIDTH), BF16),
        sds((bsz, s, 2 * HGRN_WIDTH), F32), sds((bsz, s, HGRN_WIDTH), F32),
    )
    out_specs = (head4, head4, vt, tok(256), tok(256), vt,
                 tok(HGRN_WIDTH), tok(HGRN_WIDTH), tok(2 * HGRN_WIDTH), tok(HGRN_WIDTH))
    return pl.pallas_call(
        _pre_kernel, grid=grid, in_specs=in_specs, out_specs=out_specs, out_shape=out_shape,
        compiler_params=_params(2), name='pre_mixer',
    )(xall, mods3, g1, win, gq, wuq, gkv, wukv, llb, l1mlb, *tables)


def _attend(streams, n_groups, tkg, dv, s_ref):
    tq = s_ref.shape[-1]

    def score(g, slot):
        maxes = []
        for si, (q, key_rows, _) in enumerate(streams):
            s = _dot_nt(key_rows(g), q)
            s_ref[si, slot, 0:tkg, :] = s
            maxes.append(jnp.max(s, axis=0, keepdims=True))
        return maxes

    def softmax_pv(g, slot, carry, maxes):
        new = []
        for si, ((_, _, value_t), (m, l, acc)) in enumerate(zip(streams, carry)):
            m_new = jnp.maximum(m, maxes[si])
            alpha = jnp.exp2(m - m_new)
            p = jnp.exp2(s_ref[si, slot, 0:tkg, :] - m_new)
            l = alpha * l + jnp.sum(p, axis=0, keepdims=True)
            acc = alpha * acc + _dot(value_t(g), p.astype(BF16))
            new.append((m_new, l, acc))
        return tuple(new)

    def body(t, state):
        carry, max0 = state
        g = 2 * t
        max1 = score(g + 1, 1)
        carry = softmax_pv(g, 0, carry, max0)
        max0 = score(g + 2, 0)
        return softmax_pv(g + 1, 1, carry, max1), max0

    carry = tuple((jnp.full((1, tq), -jnp.inf, F32), jnp.zeros((1, tq), F32), jnp.zeros((dv, tq), F32))
                  for _ in streams)
    max0 = score(0, 0)
    n_pairs = (n_groups - 1) // 2
    if n_pairs > 0:
        carry, max0 = lax.fori_loop(0, n_pairs, body, (carry, max0), unroll=True)
    g = 2 * n_pairs
    if n_groups % 2 == 0:
        max1 = score(g + 1, 1)
        carry = softmax_pv(g, 0, carry, max0)
        carry = softmax_pv(g + 1, 1, carry, max1)
    else:
        carry = softmax_pv(g, 0, carry, max0)
    return [(acc, l) for (_, l, acc) in carry]


def _chunks_per_group(n_chunks):
    return next(u for u in (3, 4, 2, 1) if n_chunks % u == 0)


def _for_context_and_latent(i, ctx_chunks, all_chunks, run):
    @pl.when(i < ctx_chunks)
    def _():
        u = _chunks_per_group(ctx_chunks)
        run(ctx_chunks // u, u)

    @pl.when(i >= ctx_chunks)
    def _():
        u = _chunks_per_group(all_chunks)
        run(all_chunks // u, u)


def _key_rows(k_ref, lead, tk, u):
    return lambda g: k_ref[lead + (pl.ds(pl.multiple_of(g * (u * tk), u * tk), u * tk), slice(None))]


def _value_t(vt_ref, u, hh, dv):
    def get(g):
        parts = [vt_ref[0, 0, g * u + c, hh * dv:(hh + 1) * dv, :] for c in range(u)]
        return parts[0] if u == 1 else jnp.concatenate(parts, axis=1)
    return get


def _score_scratch(n_streams, ctx_chunks, all_chunks, tk, tq):
    rows = max(_chunks_per_group(ctx_chunks), _chunks_per_group(all_chunks)) * tk
    return pltpu.VMEM((n_streams, 2, rows, tq), F32)


def _mla_kernel(ctx_chunks, q_ref, k_ref, vt_ref, o_ref, s_ref):
    tq = q_ref.shape[2]
    tk = vt_ref.shape[-1]

    def run(n_groups, u):
        streams = [(q_ref[0, hh], _key_rows(k_ref, (0, hh), tk, u), _value_t(vt_ref, u, hh, MLA_V))
                   for hh in range(2)]
        (acc0, l0), (acc1, l1) = _attend(streams, n_groups, u * tk, MLA_V, s_ref)
        ot = jnp.concatenate([acc0 / l0, acc1 / l1], axis=0)
        o_ref[0] = ot.T.astype(o_ref.dtype)

    _for_context_and_latent(pl.program_id(2), ctx_chunks, vt_ref.shape[2], run)


def _mla_attention(qm, km, vtm, ctx_chunks):
    bsz, _, s, _ = qm.shape
    nc, tk = vtm.shape[2], vtm.shape[4]
    tq = TOKEN_TILE
    return pl.pallas_call(
        functools.partial(_mla_kernel, ctx_chunks),
        grid=(bsz, MLA_HEADS // 2, s // tq),
        in_specs=[pl.BlockSpec((1, 2, tq, LANES), lambda b, p, i: (b, p, i, 0)),
                  pl.BlockSpec((1, 2, s, LANES), lambda b, p, i: (b, p, 0, 0)),
                  pl.BlockSpec((1, 1, nc, LANES, tk), lambda b, p, i: (b, p, 0, 0, 0))],
        out_specs=pl.BlockSpec((1, tq, LANES), lambda b, p, i: (b, i, p)),
        out_shape=jax.ShapeDtypeStruct((bsz, s, MLA_HEADS * MLA_V), BF16),
        scratch_shapes=[_score_scratch(2, ctx_chunks, nc, tk, tq)],
        compiler_params=_params(3), name='mla_attention',
    )(qm, km, vtm)


def _diff_kernel(ctx_chunks, lam_init, q_ref, k_ref, vt_ref, lam_ref, g_ref, o_ref, s_ref):
    tq = q_ref.shape[1]
    tk = vt_ref.shape[-1]
    pair = pl.program_id(1)

    def run(n_groups, u):
        lp = lam_ref[...]
        lam = (jnp.exp(jnp.sum(lp[0:1] * lp[1:2], axis=1, keepdims=True))
               - jnp.exp(jnp.sum(lp[2:3] * lp[3:4], axis=1, keepdims=True)) + lam_init)
        q = q_ref[0]
        lane = lax.broadcasted_iota(jnp.int32, q.shape, 1)
        key_rows = _key_rows(k_ref, (0,), tk, u)
        streams = []
        for hh in range(2):
            for mp in range(2):
                base = ((2 * pair + hh) * 2 + mp) * DIFF_DK
                qsel = jnp.where((lane >= base) & (lane < base + DIFF_DK), q, jnp.zeros_like(q))
                streams.append((qsel, key_rows, _value_t(vt_ref, u, hh, DIFF_DV)))
        res = _attend(streams, n_groups, u * tk, DIFF_DV, s_ref)
        outs = []
        for hh in range(2):
            (a1, l1), (a2, l2) = res[2 * hh], res[2 * hh + 1]
            outs.append(a1 / l1 - lam * (a2 / l2))
        normed = [o * lax.rsqrt(jnp.mean(o * o, axis=0, keepdims=True) + EPS) for o in outs]
        y = jnp.concatenate(normed, axis=0) * g_ref[...] * (1 - lam_init)
        o_ref[0] = y.T.astype(o_ref.dtype)

    _for_context_and_latent(pl.program_id(2), ctx_chunks, vt_ref.shape[2], run)


def _diff_attention(dq, dk, vtd, lam_params, g_col, ctx_chunks, lam_init):
    bsz, s, w = dq.shape
    nc, tk = vtd.shape[2], vtd.shape[4]
    tq = TOKEN_TILE
    return pl.pallas_call(
        functools.partial(_diff_kernel, ctx_chunks, lam_init),
        grid=(bsz, DIFF_HEADS // 2, s // tq),
        in_specs=[pl.BlockSpec((1, tq, w), lambda b, p, i: (b, i, 0)),
                  pl.BlockSpec((1, s, w), lambda b, p, i: (b, 0, 0)),
                  pl.BlockSpec((1, 1, nc, LANES, tk), lambda b, p, i: (b, p, 0, 0, 0)),
                  pl.BlockSpec(lam_params.shape, lambda b, p, i: (0, 0)),
                  pl.BlockSpec(g_col.shape, lambda b, p, i: (0, 0))],
        out_specs=pl.BlockSpec((1, tq, LANES), lambda b, p, i: (b, i, p)),
        out_shape=jax.ShapeDtypeStruct((bsz, s, DIFF_HEADS * DIFF_DV), BF16),
        scratch_shapes=[_score_scratch(4, ctx_chunks, nc, tk, tq)],
        compiler_params=_params(3), name='diff_attention',
    )(dq, dk, vtd, lam_params, g_col)


def _hgrn_kernel(reverse, q_ref, v_ref, lf_ref, o_ref, st_ref):
    c = SCAN_CHUNK
    tile, width = q_ref.shape[1], q_ref.shape[2]
    n_ch, pairs = tile // c, width // LANES
    log_c = c.bit_length() - 1
    tok = lax.broadcasted_iota(jnp.int32, (c, 2 * c), 0)
    src = lax.broadcasted_iota(jnp.int32, (c, 2 * c), 1) & (c - 1)
    pos_t = (c - 1 - tok) if reverse else tok
    pos_s = (c - 1 - src) if reverse else src
    tok_col = lax.broadcasted_iota(jnp.int32, (tile, 1), 0) & (c - 1)
    pos_col = (c - 1 - tok_col) if reverse else tok_col
    head_a_all = (lax.broadcasted_iota(jnp.int32, (tile, width), 1) & (LANES - 1)) < HGRN_DK
    r2 = lax.broadcasted_iota(jnp.int32, (LANES, LANES), 0)
    c2 = lax.broadcasted_iota(jnp.int32, (LANES, LANES), 1)
    same_head = (r2 < HGRN_DV) == (c2 < HGRN_DK)
    tok2 = lax.broadcasted_iota(jnp.int32, (tile, tile), 0)
    src2 = lax.broadcasted_iota(jnp.int32, (tile, tile), 1)
    tri = (((tok2 >> log_c) == (src2 >> log_c)) & ((src2 >= tok2) if reverse else (src2 <= tok2))).astype(BF16)
    earlier = lambda x, n: pltpu.roll(x, (tile - n) if reverse else n, 0)
    later_by = lambda x, n: pltpu.roll(x, n if reverse else (tile - n), 0)

    @pl.when(pl.program_id(1) == 0)
    def _():
        st_ref[...] = jnp.zeros_like(st_ref)

    rows = lambda ci: slice(ci * c, (ci + 1) * c)
    lanes = lambda p: slice(p * LANES, (p + 1) * LANES)
    cells = [(ci, p) for ci in range(n_ch) for p in range(pairs)]

    q, v, lf = q_ref[0], v_ref[0], lf_ref[0]
    p0 = lf.astype(BF16)
    r0 = lf - p0.astype(F32)
    p1 = r0.astype(BF16)
    p2 = (r0 - p1.astype(F32)).astype(BF16)
    cum = (_dot(tri, p0) + _dot(tri, p1) + _dot(tri, p2)) * LOG2_E
    k = 1.0 - jnp.exp(lf)
    zero = jnp.zeros((), BF16)
    q16 = q.astype(BF16)
    k16 = k.astype(BF16)
    ka, kb = jnp.where(head_a_all, k16, zero), jnp.where(head_a_all, zero, k16)
    va, vb = jnp.where(head_a_all, v, zero), jnp.where(head_a_all, zero, v)

    scores = {cell: None for cell in cells}

    def add_stage(qd, kda, kdb, keep):
        for ci, p in cells:
            kd2 = jnp.concatenate([kda[rows(ci), lanes(p)], kdb[rows(ci), lanes(p)]], axis=0)
            sc = jnp.where(keep, _dot_nt(qd[rows(ci), lanes(p)], kd2), 0.0)
            scores[ci, p] = sc if scores[ci, p] is None else scores[ci, p] + sc

    end = cum
    block = 2
    while block <= c:
        half = block // 2
        first = (pos_col & (block - 1)) < half
        ref = jnp.where(first, end, earlier(end, half))
        end = jnp.where(first, later_by(end, half), end)
        e = jnp.exp2(-jnp.abs(cum - ref)).astype(BF16)
        e_q = jnp.where(first, zero, e)
        e_k = jnp.where(first, e, zero)
        shift = block.bit_length() - 1
        add_stage(q16 * e_q, ka * e_k, kb * e_k, (pos_t >> shift) == (pos_s >> shift))
        block *= 2
    add_stage(q16, ka, kb, tok == src)
    tot = end
    qs = (q * jnp.exp2(cum)).astype(BF16)
    ks = (k * jnp.exp2(tot - cum)).astype(BF16)

    intra, upd = {}, {}
    for ci, p in cells:
        v2 = jnp.concatenate([va[rows(ci), lanes(p)], vb[rows(ci), lanes(p)]], axis=0)
        intra[ci, p] = _dot(scores[ci, p].astype(BF16), v2)
        vt = v[rows(ci), lanes(p)].astype(F32).T.astype(BF16)
        upd[ci, p] = jnp.where(same_head, _dot(vt, ks[rows(ci), lanes(p)]), 0.0)

    states = [st_ref[p] for p in range(pairs)]
    for ci in (range(n_ch - 1, -1, -1) if reverse else range(n_ch)):
        decay = jnp.exp2(tot[ci * c:ci * c + 1, :])
        outs = []
        for p in range(pairs):
            outs.append(_dot_nt(qs[rows(ci), lanes(p)], states[p].astype(BF16)) + intra[ci, p])
            states[p] = states[p] * decay[:, lanes(p)] + upd[ci, p]
        o_ref[0, rows(ci), :] = jnp.concatenate(outs, axis=1)
    for p in range(pairs):
        st_ref[p] = states[p]


def _hgrn_scan(hq, hv, lf, reverse, ctx_tiles):
    bsz, s, w = hq.shape
    tile = TOKEN_TILE
    ns = s // tile
    if reverse:
        blk = lambda i: jnp.where(i < ctx_tiles, ctx_tiles - 1 - i, ns - 1 + ctx_tiles - i)
    else:
        blk = lambda i: i
    spec = lambda lane_block: pl.BlockSpec((1, tile, w), lambda b, i: (b, blk(i), lane_block))
    return pl.pallas_call(
        functools.partial(_hgrn_kernel, reverse),
        grid=(bsz, ns),
        in_specs=[spec(0), spec(0), spec(1 if reverse else 0)],
        out_specs=spec(0),
        out_shape=jax.ShapeDtypeStruct((bsz, s, w), F32),
        scratch_shapes=[pltpu.VMEM((w // LANES, LANES, LANES), F32)],
        compiler_params=_params(2), name='hgrn_bwd' if reverse else 'hgrn_fwd',
    )(hq, hv, lf)


def _post_kernel(ffn_chunk, x_ref, mod_ref, mla_ref, diff_ref, of_ref, ob_ref, hg_ref, gh_ref, seg_ref,
                 wo_ref, g2_ref, wg_ref, wu_ref, wd_ref, o_ref):
    d = x_ref.shape[-1]
    x = x_ref[0]
    mod = mod_ref[0]
    gt1, sh2, sc2, gt2 = (mod[:, 2 * d:3 * d], mod[:, 3 * d:4 * d], mod[:, 4 * d:5 * d], mod[:, 5 * d:6 * d])
    o = of_ref[0] + ob_ref[0]
    sq = o * o
    sq_hi = sq.astype(BF16)
    sq_lo = (sq - sq_hi.astype(F32)).astype(BF16)
    seg = seg_ref[...]
    ms = (_dot(sq_hi, seg) + _dot(sq_lo, seg)) * (1.0 / HGRN_DV)
    hn = (o * lax.rsqrt(ms + EPS) * gh_ref[...] * _silu(hg_ref[0])).astype(BF16)
    na, nb = mla_ref.shape[-1], diff_ref.shape[-1]
    y = (_dot(mla_ref[0], wo_ref[0:na, :]) + _dot(diff_ref[0], wo_ref[na:na + nb, :])
         + _dot(hn, wo_ref[na + nb:, :]))
    x1 = x + gt1 * y
    h2 = (_rms(x1, g2_ref[...]) * (1 + sc2) + sh2).astype(BF16)
    dff = wg_ref.shape[1]
    acc = jnp.zeros_like(x1)
    for lo in range(0, dff, ffn_chunk):
        g = _dot(h2, wg_ref[:, lo:lo + ffn_chunk])
        u = _dot(h2, wu_ref[:, lo:lo + ffn_chunk])
        acc = acc + _dot((_silu(g) * u).astype(BF16), wd_ref[lo:lo + ffn_chunk, :])
    o_ref[0] = x1 + gt2 * acc


def _post_mixer(xall, mods3, ctx_tiles, mla, diff, o_f, o_b, hg, gh, seg, wo, g2, wg, wu, wd):
    bsz, s, d = xall.shape
    tm = TOKEN_TILE
    dff = wg.shape[1]
    ffn_chunk = dff // 2
    tok = lambda w: pl.BlockSpec((1, tm, w), lambda b, i: (b, i, 0))
    in_specs = [
        tok(d),
        _mod_spec(mods3, ctx_tiles),
        tok(mla.shape[-1]), tok(diff.shape[-1]), tok(HGRN_WIDTH), tok(HGRN_WIDTH), tok(HGRN_WIDTH),
        _const_spec(gh.shape), _const_spec(seg.shape), _const_spec(wo.shape), _const_spec(g2.shape),
        _const_spec(wg.shape), _const_spec(wu.shape), _const_spec(wd.shape),
    ]
    return pl.pallas_call(
        functools.partial(_post_kernel, ffn_chunk),
        grid=(bsz, s // tm), in_specs=in_specs, out_specs=tok(d),
        out_shape=jax.ShapeDtypeStruct((bsz, s, d), F32),
        compiler_params=_params(2), name='post_mixer',
    )(xall, mods3, mla, diff, o_f, o_b, hg, gh, seg, wo, g2, wg, wu, wd)


def _final_kernel(x_ref, g_ref, o_ref):
    o_ref[0] = _rms(x_ref[0], g_ref[...])


def _final_norm(xall, g, ctx_tiles, seq):
    bsz, _, d = xall.shape
    tm = TOKEN_TILE
    return pl.pallas_call(
        _final_kernel,
        grid=(bsz, seq // tm),
        in_specs=[pl.BlockSpec((1, tm, d), lambda b, i: (b, i + ctx_tiles, 0)),
                  pl.BlockSpec(g.shape, lambda b, i: (0, 0))],
        out_specs=pl.BlockSpec((1, tm, d), lambda b, i: (b, i, 0)),
        out_shape=jax.ShapeDtypeStruct((bsz, seq, d), F32),
        compiler_params=_params(2), name='final_norm',
    )(xall, g)


def _split_cols(w, sizes):
    return jnp.split(w, np.cumsum(sizes)[:-1].tolist(), axis=-1)


def _swap32(w):
    lead = w.shape[:-1]
    return w.reshape(*lead, -1, ROPE_DIM)[..., _ROPE_SWAP].reshape(*lead, -1)


def _layout_weights(w_in, w_uq, w_ukv):
    depth, d, _ = w_in.shape
    in_sizes = (MLA_Q_RANK, MLA_KV_RANK, MLA_ROPE, 256, 256, 256,
                HGRN_WIDTH, HGRN_WIDTH, HGRN_WIDTH, HGRN_WIDTH, HGRN_WIDTH)
    cq, ckv, kr, dq, dk, dv, hq, hff, hfb, hi, hg = _split_cols(w_in, in_sizes)
    zeros = lambda rows, n: jnp.zeros((depth, rows, n), F32)
    rope_tile = lambda w: jnp.concatenate([zeros(w.shape[1], MLA_NOPE), w, zeros(w.shape[1], LANES - MLA_NOPE - MLA_ROPE)], -1)
    dqs = dq * (DIFF_SCALE * LOG2_E)
    win = jnp.concatenate([cq, ckv, rope_tile(kr), rope_tile(_swap32(kr)), dqs, _swap32(dqs), dk, _swap32(dk), dv,
                           hq, hff, hfb, hi, hg], axis=-1).astype(BF16)
    per = MLA_NOPE + MLA_ROPE
    uq_tiles, uq_sw_tiles = [], []
    for h in range(MLA_HEADS):
        nope = w_uq[..., h * per:h * per + MLA_NOPE]
        rope = w_uq[..., h * per + MLA_NOPE:(h + 1) * per]
        uq_tiles.append(jnp.concatenate([nope, rope, zeros(MLA_Q_RANK, LANES - per)], -1))
        uq_sw_tiles.append(rope_tile(_swap32(rope)))
    wuq = (jnp.concatenate(uq_tiles + uq_sw_tiles, axis=-1) * (MLA_SCALE * LOG2_E)).astype(BF16)
    kv = MLA_NOPE + MLA_V
    k_tiles = [jnp.concatenate([w_ukv[..., h * kv:h * kv + MLA_NOPE], zeros(MLA_KV_RANK, LANES - MLA_NOPE)], -1)
               for h in range(MLA_HEADS)]
    v_cols = [w_ukv[..., h * kv + MLA_NOPE:(h + 1) * kv] for h in range(MLA_HEADS)]
    wukv = jnp.concatenate(k_tiles + v_cols, axis=-1).astype(BF16)
    return win, wuq, wukv


def _rope_tables(seq, ctx_len):
    t = jnp.arange(seq)
    row = (t // GRID_W).astype(F32)
    col = (t % GRID_W).astype(F32)
    n_freq = ROPE_DIM // 4
    freqs = ROPE_BASE ** (-jnp.arange(n_freq, dtype=F32) / n_freq)
    ang_r = row[:, None] * freqs
    ang_c = col[:, None] * freqs
    cr, sr, cc, sc = jnp.cos(ang_r), jnp.sin(ang_r), jnp.cos(ang_c), jnp.sin(ang_c)
    c32 = jnp.concatenate([cr, cr, cc, cc], axis=-1)
    s32 = jnp.concatenate([-sr, sr, -sc, sc], axis=-1)
    c32 = jnp.concatenate([jnp.ones((ctx_len, ROPE_DIM), F32), c32], axis=0)
    s32 = jnp.concatenate([jnp.zeros((ctx_len, ROPE_DIM), F32), s32], axis=0)
    n = c32.shape[0]
    cd = jnp.tile(c32, (1, LANES // ROPE_DIM))
    sd = jnp.tile(s32, (1, LANES // ROPE_DIM))
    pad = jnp.zeros((n, LANES - MLA_NOPE - MLA_ROPE), F32)
    cq = jnp.concatenate([jnp.ones((n, MLA_NOPE), F32), c32, pad], axis=-1)
    sq = jnp.concatenate([jnp.zeros((n, MLA_NOPE), F32), s32, pad], axis=-1)
    return cd, sd, cq, sq


def kernel(x, c, ctx, c_ctx, w_ada, b_ada, g_norm1, g_norm2, w_in, g_q_norm, w_uq, g_kv_norm, w_ukv,
           diff_lambda, g_diff_norm, hgrn_lower_bounds, g_hgrn_norm, w_out, w_ffn_gate, w_ffn_up,
           w_ffn_down, g_final):
    bsz, seq, d = x.shape
    ctx_len = ctx.shape[1]
    depth = w_in.shape[0]
    assert ctx_len % TOKEN_TILE == 0 and seq % TOKEN_TILE == 0 and seq % GRID_W == 0
    ctx_tiles = ctx_len // TOKEN_TILE

    cvec = jnp.concatenate([c, c_ctx[None, :], jnp.zeros((8 - bsz - 1, d), F32)], axis=0)
    mods = _adaln(cvec.T, bsz + 1, w_ada, b_ada)
    win, wuq, wukv = _layout_weights(w_in, w_uq, w_ukv)
    wo, wg, wu, wd = (w.astype(BF16) for w in (w_out, w_ffn_gate, w_ffn_up, w_ffn_down))
    tables = _rope_tables(seq, ctx_len)

    p = jax.nn.softmax(hgrn_lower_bounds.astype(F32), axis=0)
    cum = jnp.cumsum(p, axis=0)
    lb = (cum - cum[0:1]).reshape(depth, 1, 2 * HGRN_WIDTH)
    llb, l1mlb = jnp.log(lb), jnp.log1p(-lb)

    idx = np.arange(HGRN_WIDTH)
    seg = jnp.asarray(idx[:, None] // HGRN_DV == idx[None, :] // HGRN_DV, BF16)

    xall = jnp.concatenate([ctx, x], axis=1)
    for l in range(depth):
        lam_init = 0.8 - 0.6 * math.exp(-0.3 * l)
        mods3 = mods[l, 0:bsz + 1].reshape(bsz + 1, 1, 6 * d)
        qm, km, vtm, dq, dk, vtd, hq, hv, lf, hg = _pre_mixer(
            xall, mods3, ctx_tiles, g_norm1[l][None], win[l], g_q_norm[l][None], wuq[l], g_kv_norm[l][None], wukv[l],
            llb[l], l1mlb[l], tables)
        mla = _mla_attention(qm, km, vtm, ctx_tiles)
        g_col = jnp.tile(g_diff_norm[l], 2)[:, None]
        diff = _diff_attention(dq, dk, vtd, diff_lambda[l].astype(F32), g_col, ctx_tiles, lam_init)
        o_f = _hgrn_scan(hq, hv, lf, False, ctx_tiles)
        o_b = _hgrn_scan(hq, hv, lf, True, ctx_tiles)
        gh = jnp.tile(g_hgrn_norm[l], HGRN_HEADS)[None]
        xall = _post_mixer(xall, mods3, ctx_tiles, mla, diff, o_f, o_b, hg, gh, seg, wo[l], g_norm2[l][None],
                           wg[l], wu[l], wd[l])
    return _final_norm(xall, g_final[None], ctx_tiles, seq)
```

```python
import functools
import math

import jax
import jax.numpy as jnp
import numpy as np
from jax import lax
from jax.experimental import pallas as pl
from jax.experimental.pallas import tpu as pltpu

F32 = jnp.float32
BF16 = jnp.bfloat16

GRID_W = 64
EPS = 1e-6
ROPE_BASE = 10000.0
ROPE_DIM = 32

MLA_HEADS = 4
MLA_Q_RANK = 256
MLA_KV_RANK = 128
MLA_NOPE = 64
MLA_ROPE = ROPE_DIM
MLA_V = 64
MLA_SCALE = (MLA_NOPE + MLA_ROPE) ** -0.5

DIFF_HEADS = 4
DIFF_DK = ROPE_DIM
DIFF_DV = 2 * DIFF_DK
DIFF_SCALE = DIFF_DK ** -0.5
LOG2_E = math.log2(math.e)

HGRN_HEADS = 8
HGRN_DK = 64
HGRN_DV = 64
HGRN_WIDTH = HGRN_HEADS * HGRN_DV

LANES = 128
BF16_SUBLANES = 16
MXU_DIM = 256
VMEM_LIMIT_BYTES = 56 * 1024 * 1024

TOKEN_TILE = MXU_DIM
SCAN_CHUNK = 64

_IN_GROUPS = (
    ('cq', MLA_Q_RANK), ('ckv', MLA_KV_RANK), ('kr', LANES), ('kr_sw', LANES),
    ('dq', 256), ('dq_sw', 256), ('dk', 256), ('dk_sw', 256), ('dv', 256),
    ('hq', HGRN_WIDTH), ('hf', 2 * HGRN_WIDTH), ('hi', HGRN_WIDTH), ('hg', HGRN_WIDTH),
)
_IN_OFF = {}
_o = 0
for _n, _w in _IN_GROUPS:
    _IN_OFF[_n] = (_o, _o + _w)
    _o += _w
IN_WIDTH_PADDED = _o

_ROPE_SWAP = np.concatenate([np.arange(8, 16), np.arange(0, 8), np.arange(24, 32), np.arange(16, 24)])


def _silu(x):
    return x * jax.nn.sigmoid(x)


def _rms(x, g):
    return x * lax.rsqrt(jnp.mean(x * x, axis=-1, keepdims=True) + EPS) * g


def _dot(a, b):
    return jnp.dot(a, b, preferred_element_type=F32)


def _dot_nt(a, b):
    return lax.dot_general(a, b, (((1,), (1,)), ((), ())), preferred_element_type=F32)


def _params(n_grid):
    return pltpu.CompilerParams(dimension_semantics=('arbitrary',) * n_grid,
                                vmem_limit_bytes=VMEM_LIMIT_BYTES)


def _const_spec(shape):
    nd = len(shape)
    return pl.BlockSpec(shape, lambda *_: (0,) * nd, pipeline_mode=pl.Buffered(1))


def _layer_spec(stacked, layer):
    tail = stacked.shape[1:]
    return pl.BlockSpec((None,) + tail, lambda *_: (layer,) + (0,) * len(tail), pipeline_mode=pl.Buffered(1))


def _adaln_kernel(n_rows, cvt_ref, w_ref, b_ref, o_ref):
    s = _silu(cvt_ref[...])
    w = w_ref[0]
    outs = [jnp.sum(s[:, r:r + 1] * w, axis=0, keepdims=True) + b_ref[0] for r in range(n_rows)]
    outs.append(jnp.zeros((o_ref.shape[1] - n_rows, w.shape[1]), F32))
    o_ref[0] = jnp.concatenate(outs, axis=0)


def _adaln(cvec_t, n_rows, w_ada, b_ada):
    depth, d, n = w_ada.shape
    tn = 1536
    return pl.pallas_call(
        functools.partial(_adaln_kernel, n_rows),
        grid=(depth, n // tn),
        in_specs=[pl.BlockSpec((d, 8), lambda l, j: (0, 0)),
                  pl.BlockSpec((1, d, tn), lambda l, j: (l, 0, j)),
                  pl.BlockSpec((1, 1, tn), lambda l, j: (l, 0, j))],
        out_specs=pl.BlockSpec((1, 8, tn), lambda l, j: (l, 0, j)),
        out_shape=jax.ShapeDtypeStruct((depth, 8, n), F32),
        compiler_params=_params(2),
        name='adaln',
    )(cvec_t, w_ada, b_ada.reshape(depth, 1, n))


def _pre_kernel(x_ref, mod_ref, g1_ref, win_ref, gq_ref, wuq_ref, gkv_ref, wukv_ref, llb_ref, l1mlb_ref,
                cd_ref, sd_ref, cq_ref, sq_ref,
                qm_ref, km_ref, vtm_ref, dq_ref, dk_ref, vtd_ref, hq_ref, hv_ref, lf_ref, hg_ref):
    d = x_ref.shape[-1]
    x = x_ref[0]
    mod = mod_ref[0]
    sh1, sc1 = mod[:, 0:d], mod[:, d:2 * d]
    hb = (_rms(x, g1_ref[...]) * (1 + sc1) + sh1).astype(BF16)

    proj_all = _dot(hb, win_ref[...])

    def proj(name):
        lo, hi = _IN_OFF[name]
        return proj_all[:, lo:hi]

    cq_t, sq_t = cq_ref[...], sq_ref[...]
    cqn = _rms(proj('cq'), gq_ref[...]).astype(BF16)
    uq = _dot(cqn, wuq_ref[...])
    for h in range(MLA_HEADS):
        a = uq[:, h * LANES:(h + 1) * LANES]
        b = uq[:, (MLA_HEADS + h) * LANES:(MLA_HEADS + h + 1) * LANES]
        qm_ref[0, h] = (a * cq_t + b * sq_t).astype(BF16)
    ckvn = _rms(proj('ckv'), gkv_ref[...]).astype(BF16)
    ukv = _dot(ckvn, wukv_ref[...])
    krt = proj('kr') * cq_t + proj('kr_sw') * sq_t
    for h in range(MLA_HEADS):
        km_ref[0, h] = (ukv[:, h * LANES:(h + 1) * LANES] + krt).astype(BF16)
    for p in range(MLA_HEADS // 2):
        lo = (MLA_HEADS + p) * LANES
        vtm_ref[0, p, 0] = ukv[:, lo:lo + LANES].T.astype(BF16)
    cd2 = jnp.concatenate([cd_ref[...]] * 2, axis=1)
    sd2 = jnp.concatenate([sd_ref[...]] * 2, axis=1)
    dq_ref[0] = (proj('dq') * cd2 + proj('dq_sw') * sd2).astype(BF16)
    dk_ref[0] = (proj('dk') * cd2 + proj('dk_sw') * sd2).astype(BF16)
    dv = proj('dv')
    for p in range(DIFF_HEADS // 2):
        vtd_ref[0, p, 0] = dv[:, p * LANES:(p + 1) * LANES].T.astype(BF16)
    hq_ref[0] = _silu(proj('hq'))
    z = proj('hf')
    ls = jnp.minimum(z, 0.0) - jnp.log1p(jnp.exp(-jnp.abs(z)))
    a = llb_ref[...]
    b = l1mlb_ref[...] + ls
    lf_ref[0] = jnp.maximum(a, b) + jnp.log1p(jnp.exp(-jnp.abs(a - b)))
    hv_ref[0] = proj('hi').astype(BF16)
    hg_ref[0] = proj('hg')


def _mod_spec(mods3, ctx_tiles):
    ctx_row = mods3.shape[0] - 1
    return pl.BlockSpec((1, 1, mods3.shape[-1]), lambda b, i: (jnp.where(i < ctx_tiles, ctx_row, b), 0, 0))


def _pre_mixer(xall, mods3, ctx_tiles, layer, g1, win, gq, wuq, gkv, wukv, llb, l1mlb, tables):
    bsz, s, d = xall.shape
    tm = TOKEN_TILE
    ns = s // tm
    grid = (bsz, ns)
    tok = lambda w: pl.BlockSpec((1, tm, w), lambda b, i: (b, i, 0))
    tab = pl.BlockSpec((tm, LANES), lambda b, i: (i, 0))
    head4 = pl.BlockSpec((1, MLA_HEADS, tm, LANES), lambda b, i: (b, 0, i, 0))
    vt = pl.BlockSpec((1, 2, 1, LANES, tm), lambda b, i: (b, 0, i, 0, 0))
    in_specs = [
        tok(d),
        _mod_spec(mods3, ctx_tiles),
        _const_spec(g1.shape), _layer_spec(win, layer), _const_spec(gq.shape), _layer_spec(wuq, layer),
        _const_spec(gkv.shape), _layer_spec(wukv, layer), _const_spec(llb.shape), _const_spec(l1mlb.shape),
        tab, tab, tab, tab,
    ]
    sds = jax.ShapeDtypeStruct
    out_shape = (
        sds((bsz, MLA_HEADS, s, LANES), BF16), sds((bsz, MLA_HEADS, s, LANES), BF16),
        sds((bsz, 2, ns, LANES, tm), BF16),
        sds((bsz, s, 256), BF16), sds((bsz, s, 256), BF16), sds((bsz, 2, ns, LANES, tm), BF16),
        sds((bsz, s, HGRN_WIDTH), F32), sds((bsz, s, HGRN_WIDTH), BF16),
        sds((bsz, s, 2 * HGRN_WIDTH), F32), sds((bsz, s, HGRN_WIDTH), F32),
    )
    out_specs = (head4, head4, vt, tok(256), tok(256), vt,
                 tok(HGRN_WIDTH), tok(HGRN_WIDTH), tok(2 * HGRN_WIDTH), tok(HGRN_WIDTH))
    return pl.pallas_call(
        _pre_kernel, grid=grid, in_specs=in_specs, out_specs=out_specs, out_shape=out_shape,
        compiler_params=_params(2), name='pre_mixer',
    )(xall, mods3, g1, win, gq, wuq, gkv, wukv, llb, l1mlb, *tables)


def _attend(streams, n_groups, tkg, dv, s_ref):
    tq = s_ref.shape[-1]

    def score(g, slot):
        maxes = []
        for si, (q, key_rows, _) in enumerate(streams):
            s = _dot_nt(key_rows(g), q)
            s_ref[si, slot, 0:tkg, :] = s
            maxes.append(jnp.max(s, axis=0, keepdims=True))
        return maxes

    ones_rows = (lax.broadcasted_iota(jnp.int32, (BF16_SUBLANES, tkg), 0) == 0).astype(BF16)

    def softmax_pv(g, slot, carry, maxes):
        new = []
        for si, ((_, _, value_t), (m, acc)) in enumerate(zip(streams, carry)):
            m_new = jnp.maximum(m, maxes[si])
            alpha = jnp.exp2(m - m_new)
            p = jnp.exp2(s_ref[si, slot, 0:tkg, :] - m_new).astype(BF16)
            v1 = jnp.concatenate([value_t(g), ones_rows], axis=0)
            acc = alpha * acc + _dot(v1, p)
            new.append((m_new, acc))
        return tuple(new)

    def body(t, state):
        carry, max0 = state
        g = 2 * t
        max1 = score(g + 1, 1)
        carry = softmax_pv(g, 0, carry, max0)
        max0 = score(g + 2, 0)
        return softmax_pv(g + 1, 1, carry, max1), max0

    carry = tuple((jnp.full((1, tq), -jnp.inf, F32), jnp.zeros((dv + BF16_SUBLANES, tq), F32))
                  for _ in streams)
    max0 = score(0, 0)
    n_pairs = (n_groups - 1) // 2
    if n_pairs > 0:
        carry, max0 = lax.fori_loop(0, n_pairs, body, (carry, max0), unroll=True)
    g = 2 * n_pairs
    if n_groups % 2 == 0:
        max1 = score(g + 1, 1)
        carry = softmax_pv(g, 0, carry, max0)
        carry = softmax_pv(g + 1, 1, carry, max1)
    else:
        carry = softmax_pv(g, 0, carry, max0)
    return [(acc[0:dv], acc[dv:dv + 1]) for (_, acc) in carry]


def _chunks_per_group(n_chunks):
    return next(u for u in (3, 4, 2, 1) if n_chunks % u == 0)


def _for_context_and_latent(i, ctx_chunks, all_chunks, run):
    @pl.when(i < ctx_chunks)
    def _():
        u = _chunks_per_group(ctx_chunks)
        run(ctx_chunks // u, u)

    @pl.when(i >= ctx_chunks)
    def _():
        u = _chunks_per_group(all_chunks)
        run(all_chunks // u, u)


def _key_rows(k_ref, lead, tk, u):
    return lambda g: k_ref[lead + (pl.ds(pl.multiple_of(g * (u * tk), u * tk), u * tk), slice(None))]


def _value_t(vt_ref, u, hh, dv):
    def get(g):
        parts = [vt_ref[0, 0, g * u + c, hh * dv:(hh + 1) * dv, :] for c in range(u)]
        return parts[0] if u == 1 else jnp.concatenate(parts, axis=1)
    return get


def _score_scratch(n_streams, ctx_chunks, all_chunks, tk, tq):
    rows = max(_chunks_per_group(ctx_chunks), _chunks_per_group(all_chunks)) * tk
    return pltpu.VMEM((n_streams, 2, rows, tq), F32)


def _mla_kernel(ctx_chunks, q_ref, k_ref, vt_ref, o_ref, s_ref):
    tq = q_ref.shape[2]
    tk = vt_ref.shape[-1]

    def run(n_groups, u):
        streams = [(q_ref[0, hh], _key_rows(k_ref, (0, hh), tk, u), _value_t(vt_ref, u, hh, MLA_V))
                   for hh in range(2)]
        (acc0, l0), (acc1, l1) = _attend(streams, n_groups, u * tk, MLA_V, s_ref)
        ot = jnp.concatenate([acc0 / l0, acc1 / l1], axis=0)
        o_ref[0] = ot.T.astype(o_ref.dtype)

    _for_context_and_latent(pl.program_id(2), ctx_chunks, vt_ref.shape[2], run)


def _mla_attention(qm, km, vtm, ctx_chunks):
    bsz, _, s, _ = qm.shape
    nc, tk = vtm.shape[2], vtm.shape[4]
    tq = TOKEN_TILE
    return pl.pallas_call(
        functools.partial(_mla_kernel, ctx_chunks),
        grid=(bsz, MLA_HEADS // 2, s // tq),
        in_specs=[pl.BlockSpec((1, 2, tq, LANES), lambda b, p, i: (b, p, i, 0)),
                  pl.BlockSpec((1, 2, s, LANES), lambda b, p, i: (b, p, 0, 0)),
                  pl.BlockSpec((1, 1, nc, LANES, tk), lambda b, p, i: (b, p, 0, 0, 0))],
        out_specs=pl.BlockSpec((1, tq, LANES), lambda b, p, i: (b, i, p)),
        out_shape=jax.ShapeDtypeStruct((bsz, s, MLA_HEADS * MLA_V), BF16),
        scratch_shapes=[_score_scratch(2, ctx_chunks, nc, tk, tq)],
        compiler_params=_params(3), name='mla_attention',
    )(qm, km, vtm)


def _diff_kernel(ctx_chunks, lam_init, q_ref, k_ref, vt_ref, lam_ref, g_ref, o_ref, s_ref):
    tq = q_ref.shape[1]
    tk = vt_ref.shape[-1]
    pair = pl.program_id(1)

    def run(n_groups, u):
        lp = lam_ref[...]
        lam = (jnp.exp(jnp.sum(lp[0:1] * lp[1:2], axis=1, keepdims=True))
               - jnp.exp(jnp.sum(lp[2:3] * lp[3:4], axis=1, keepdims=True)) + lam_init)
        q = q_ref[0]
        lane = lax.broadcasted_iota(jnp.int32, q.shape, 1)
        key_rows = _key_rows(k_ref, (0,), tk, u)
        streams = []
        for hh in range(2):
            for mp in range(2):
                base = ((2 * pair + hh) * 2 + mp) * DIFF_DK
                qsel = jnp.where((lane >= base) & (lane < base + DIFF_DK), q, jnp.zeros_like(q))
                streams.append((qsel, key_rows, _value_t(vt_ref, u, hh, DIFF_DV)))
        res = _attend(streams, n_groups, u * tk, DIFF_DV, s_ref)
        outs = []
        for hh in range(2):
            (a1, l1), (a2, l2) = res[2 * hh], res[2 * hh + 1]
            outs.append(a1 / l1 - lam * (a2 / l2))
        normed = [o * lax.rsqrt(jnp.mean(o * o, axis=0, keepdims=True) + EPS) for o in outs]
        y = jnp.concatenate(normed, axis=0) * g_ref[...] * (1 - lam_init)
        o_ref[0] = y.T.astype(o_ref.dtype)

    _for_context_and_latent(pl.program_id(2), ctx_chunks, vt_ref.shape[2], run)


def _diff_attention(dq, dk, vtd, lam_params, g_col, ctx_chunks, lam_init):
    bsz, s, w = dq.shape
    nc, tk = vtd.shape[2], vtd.shape[4]
    tq = TOKEN_TILE
    return pl.pallas_call(
        functools.partial(_diff_kernel, ctx_chunks, lam_init),
        grid=(bsz, DIFF_HEADS // 2, s // tq),
        in_specs=[pl.BlockSpec((1, tq, w), lambda b, p, i: (b, i, 0)),
                  pl.BlockSpec((1, s, w), lambda b, p, i: (b, 0, 0)),
                  pl.BlockSpec((1, 1, nc, LANES, tk), lambda b, p, i: (b, p, 0, 0, 0)),
                  pl.BlockSpec(lam_params.shape, lambda b, p, i: (0, 0)),
                  pl.BlockSpec(g_col.shape, lambda b, p, i: (0, 0))],
        out_specs=pl.BlockSpec((1, tq, LANES), lambda b, p, i: (b, i, p)),
        out_shape=jax.ShapeDtypeStruct((bsz, s, DIFF_HEADS * DIFF_DV), BF16),
        scratch_shapes=[_score_scratch(4, ctx_chunks, nc, tk, tq)],
        compiler_params=_params(3), name='diff_attention',
    )(dq, dk, vtd, lam_params, g_col)


def _hgrn_kernel(reverse, q_ref, v_ref, lf_ref, o_ref, st_ref):
    c = SCAN_CHUNK
    tile, width = q_ref.shape[1], q_ref.shape[2]
    n_ch, pairs = tile // c, width // LANES
    log_c = c.bit_length() - 1
    tok = lax.broadcasted_iota(jnp.int32, (c, 2 * c), 0)
    src = lax.broadcasted_iota(jnp.int32, (c, 2 * c), 1) & (c - 1)
    pos_t = (c - 1 - tok) if reverse else tok
    pos_s = (c - 1 - src) if reverse else src
    tok_col = lax.broadcasted_iota(jnp.int32, (tile, 1), 0) & (c - 1)
    pos_col = (c - 1 - tok_col) if reverse else tok_col
    head_a_all = (lax.broadcasted_iota(jnp.int32, (tile, width), 1) & (LANES - 1)) < HGRN_DK
    r2 = lax.broadcasted_iota(jnp.int32, (LANES, LANES), 0)
    c2 = lax.broadcasted_iota(jnp.int32, (LANES, LANES), 1)
    same_head = (r2 < HGRN_DV) == (c2 < HGRN_DK)
    tok2 = lax.broadcasted_iota(jnp.int32, (tile, tile), 0)
    src2 = lax.broadcasted_iota(jnp.int32, (tile, tile), 1)
    tri = (((tok2 >> log_c) == (src2 >> log_c)) & ((src2 >= tok2) if reverse else (src2 <= tok2))).astype(BF16)
    earlier = lambda x, n: pltpu.roll(x, (tile - n) if reverse else n, 0)
    later_by = lambda x, n: pltpu.roll(x, n if reverse else (tile - n), 0)

    @pl.when(pl.program_id(1) == 0)
    def _():
        st_ref[...] = jnp.zeros_like(st_ref)

    rows = lambda ci: slice(ci * c, (ci + 1) * c)
    lanes = lambda p: slice(p * LANES, (p + 1) * LANES)
    cells = [(ci, p) for ci in range(n_ch) for p in range(pairs)]

    q, v, lf = q_ref[0], v_ref[0], lf_ref[0]
    p0 = lf.astype(BF16)
    r0 = lf - p0.astype(F32)
    p1 = r0.astype(BF16)
    p2 = (r0 - p1.astype(F32)).astype(BF16)
    cum = (_dot(tri, p0) + _dot(tri, p1) + _dot(tri, p2)) * LOG2_E
    k = 1.0 - jnp.exp(lf)
    zero = jnp.zeros((), BF16)
    q16 = q.astype(BF16)
    k16 = k.astype(BF16)
    ka, kb = jnp.where(head_a_all, k16, zero), jnp.where(head_a_all, zero, k16)
    va, vb = jnp.where(head_a_all, v, zero), jnp.where(head_a_all, zero, v)

    scores = {cell: None for cell in cells}

    def add_stage(qd, kda, kdb, keep):
        for ci, p in cells:
            kd2 = jnp.concatenate([kda[rows(ci), lanes(p)], kdb[rows(ci), lanes(p)]], axis=0)
            sc = jnp.where(keep, _dot_nt(qd[rows(ci), lanes(p)], kd2), 0.0)
            scores[ci, p] = sc if scores[ci, p] is None else scores[ci, p] + sc

    end = cum
    block = 2
    while block <= c:
        half = block // 2
        first = (pos_col & (block - 1)) < half
        ref = jnp.where(first, end, earlier(end, half))
        end = jnp.where(first, later_by(end, half), end)
        e = jnp.exp2(-jnp.abs(cum - ref)).astype(BF16)
        e_q = jnp.where(first, zero, e)
        e_k = jnp.where(first, e, zero)
        shift = block.bit_length() - 1
        add_stage(q16 * e_q, ka * e_k, kb * e_k, (pos_t >> shift) == (pos_s >> shift))
        block *= 2
    add_stage(q16, ka, kb, tok == src)
    tot = end
    qs = (q * jnp.exp2(cum)).astype(BF16)
    ks = (k * jnp.exp2(tot - cum)).astype(BF16)

    intra, upd = {}, {}
    for ci, p in cells:
        v2 = jnp.concatenate([va[rows(ci), lanes(p)], vb[rows(ci), lanes(p)]], axis=0)
        intra[ci, p] = _dot(scores[ci, p].astype(BF16), v2)
        vt = v[rows(ci), lanes(p)].astype(F32).T.astype(BF16)
        upd[ci, p] = jnp.where(same_head, _dot(vt, ks[rows(ci), lanes(p)]), 0.0)

    states = [st_ref[p] for p in range(pairs)]
    for ci in (range(n_ch - 1, -1, -1) if reverse else range(n_ch)):
        decay = jnp.exp2(tot[ci * c:ci * c + 1, :])
        outs = []
        for p in range(pairs):
            outs.append(_dot_nt(qs[rows(ci), lanes(p)], states[p].astype(BF16)) + intra[ci, p])
            states[p] = states[p] * decay[:, lanes(p)] + upd[ci, p]
        o_ref[0, rows(ci), :] = jnp.concatenate(outs, axis=1)
    for p in range(pairs):
        st_ref[p] = states[p]


def _hgrn_scan(hq, hv, lf, reverse, ctx_tiles):
    bsz, s, w = hq.shape
    tile = TOKEN_TILE
    ns = s // tile
    if reverse:
        blk = lambda i: jnp.where(i < ctx_tiles, ctx_tiles - 1 - i, ns - 1 + ctx_tiles - i)
    else:
        blk = lambda i: i
    spec = lambda lane_block: pl.BlockSpec((1, tile, w), lambda b, i: (b, blk(i), lane_block))
    return pl.pallas_call(
        functools.partial(_hgrn_kernel, reverse),
        grid=(bsz, ns),
        in_specs=[spec(0), spec(0), spec(1 if reverse else 0)],
        out_specs=spec(0),
        out_shape=jax.ShapeDtypeStruct((bsz, s, w), F32),
        scratch_shapes=[pltpu.VMEM((w // LANES, LANES, LANES), F32)],
        compiler_params=_params(2), name='hgrn_bwd' if reverse else 'hgrn_fwd',
    )(hq, hv, lf)


def _post_kernel(ffn_chunk, x_ref, mod_ref, mla_ref, diff_ref, of_ref, ob_ref, hg_ref, gh_ref, seg_ref,
                 wo_ref, g2_ref, wg_ref, wu_ref, wd_ref, o_ref):
    d = x_ref.shape[-1]
    x = x_ref[0]
    mod = mod_ref[0]
    gt1, sh2, sc2, gt2 = (mod[:, 2 * d:3 * d], mod[:, 3 * d:4 * d], mod[:, 4 * d:5 * d], mod[:, 5 * d:6 * d])
    o = of_ref[0] + ob_ref[0]
    sq = o * o
    sq_hi = sq.astype(BF16)
    sq_lo = (sq - sq_hi.astype(F32)).astype(BF16)
    seg = seg_ref[...]
    ms = (_dot(sq_hi, seg) + _dot(sq_lo, seg)) * (1.0 / HGRN_DV)
    hn = (o * lax.rsqrt(ms + EPS) * gh_ref[...] * _silu(hg_ref[0])).astype(BF16)
    na, nb = mla_ref.shape[-1], diff_ref.shape[-1]
    y = (_dot(mla_ref[0], wo_ref[0:na, :]) + _dot(diff_ref[0], wo_ref[na:na + nb, :])
         + _dot(hn, wo_ref[na + nb:, :]))
    x1 = x + gt1 * y
    h2 = (_rms(x1, g2_ref[...]) * (1 + sc2) + sh2).astype(BF16)
    dff = wg_ref.shape[1]
    acc = jnp.zeros_like(x1)
    for lo in range(0, dff, ffn_chunk):
        g = _dot(h2, wg_ref[:, lo:lo + ffn_chunk])
        u = _dot(h2, wu_ref[:, lo:lo + ffn_chunk])
        acc = acc + _dot((_silu(g) * u).astype(BF16), wd_ref[lo:lo + ffn_chunk, :])
    o_ref[0] = x1 + gt2 * acc


def _post_mixer(xall, mods3, ctx_tiles, layer, mla, diff, o_f, o_b, hg, gh, seg, wo, g2, wg, wu, wd):
    bsz, s, d = xall.shape
    tm = TOKEN_TILE
    ffn_chunk = wg.shape[-1] // 2
    tok = lambda w: pl.BlockSpec((1, tm, w), lambda b, i: (b, i, 0))
    in_specs = [
        tok(d),
        _mod_spec(mods3, ctx_tiles),
        tok(mla.shape[-1]), tok(diff.shape[-1]), tok(HGRN_WIDTH), tok(HGRN_WIDTH), tok(HGRN_WIDTH),
        _const_spec(gh.shape), _const_spec(seg.shape), _layer_spec(wo, layer), _const_spec(g2.shape),
        _layer_spec(wg, layer), _layer_spec(wu, layer), _layer_spec(wd, layer),
    ]
    return pl.pallas_call(
        functools.partial(_post_kernel, ffn_chunk),
        grid=(bsz, s // tm), in_specs=in_specs, out_specs=tok(d),
        out_shape=jax.ShapeDtypeStruct((bsz, s, d), F32),
        compiler_params=_params(2), name='post_mixer',
    )(xall, mods3, mla, diff, o_f, o_b, hg, gh, seg, wo, g2, wg, wu, wd)


def _final_kernel(x_ref, g_ref, o_ref):
    o_ref[0] = _rms(x_ref[0], g_ref[...])


def _final_norm(xall, g, ctx_tiles, seq):
    bsz, _, d = xall.shape
    tm = TOKEN_TILE
    return pl.pallas_call(
        _final_kernel,
        grid=(bsz, seq // tm),
        in_specs=[pl.BlockSpec((1, tm, d), lambda b, i: (b, i + ctx_tiles, 0)),
                  pl.BlockSpec(g.shape, lambda b, i: (0, 0))],
        out_specs=pl.BlockSpec((1, tm, d), lambda b, i: (b, i, 0)),
        out_shape=jax.ShapeDtypeStruct((bsz, seq, d), F32),
        compiler_params=_params(2), name='final_norm',
    )(xall, g)


def _split_cols(w, sizes):
    return jnp.split(w, np.cumsum(sizes)[:-1].tolist(), axis=-1)


def _swap32(w):
    lead = w.shape[:-1]
    return w.reshape(*lead, -1, ROPE_DIM)[..., _ROPE_SWAP].reshape(*lead, -1)


def _layout_weights(w_in, w_uq, w_ukv):
    depth, d, _ = w_in.shape
    in_sizes = (MLA_Q_RANK, MLA_KV_RANK, MLA_ROPE, 256, 256, 256,
                HGRN_WIDTH, HGRN_WIDTH, HGRN_WIDTH, HGRN_WIDTH, HGRN_WIDTH)
    cq, ckv, kr, dq, dk, dv, hq, hff, hfb, hi, hg = _split_cols(w_in, in_sizes)
    zeros = lambda rows, n: jnp.zeros((depth, rows, n), F32)
    rope_tile = lambda w: jnp.concatenate([zeros(w.shape[1], MLA_NOPE), w, zeros(w.shape[1], LANES - MLA_NOPE - MLA_ROPE)], -1)
    dqs = dq * (DIFF_SCALE * LOG2_E)
    win = jnp.concatenate([cq, ckv, rope_tile(kr), rope_tile(_swap32(kr)), dqs, _swap32(dqs), dk, _swap32(dk), dv,
                           hq, hff, hfb, hi, hg], axis=-1).astype(BF16)
    per = MLA_NOPE + MLA_ROPE
    uq_tiles, uq_sw_tiles = [], []
    for h in range(MLA_HEADS):
        nope = w_uq[..., h * per:h * per + MLA_NOPE]
        rope = w_uq[..., h * per + MLA_NOPE:(h + 1) * per]
        uq_tiles.append(jnp.concatenate([nope, rope, zeros(MLA_Q_RANK, LANES - per)], -1))
        uq_sw_tiles.append(rope_tile(_swap32(rope)))
    wuq = (jnp.concatenate(uq_tiles + uq_sw_tiles, axis=-1) * (MLA_SCALE * LOG2_E)).astype(BF16)
    kv = MLA_NOPE + MLA_V
    k_tiles = [jnp.concatenate([w_ukv[..., h * kv:h * kv + MLA_NOPE], zeros(MLA_KV_RANK, LANES - MLA_NOPE)], -1)
               for h in range(MLA_HEADS)]
    v_cols = [w_ukv[..., h * kv + MLA_NOPE:(h + 1) * kv] for h in range(MLA_HEADS)]
    wukv = jnp.concatenate(k_tiles + v_cols, axis=-1).astype(BF16)
    return win, wuq, wukv


def _rope_tables(seq, ctx_len):
    t = jnp.arange(seq)
    row = (t // GRID_W).astype(F32)
    col = (t % GRID_W).astype(F32)
    n_freq = ROPE_DIM // 4
    freqs = ROPE_BASE ** (-jnp.arange(n_freq, dtype=F32) / n_freq)
    ang_r = row[:, None] * freqs
    ang_c = col[:, None] * freqs
    cr, sr, cc, sc = jnp.cos(ang_r), jnp.sin(ang_r), jnp.cos(ang_c), jnp.sin(ang_c)
    c32 = jnp.concatenate([cr, cr, cc, cc], axis=-1)
    s32 = jnp.concatenate([-sr, sr, -sc, sc], axis=-1)
    c32 = jnp.concatenate([jnp.ones((ctx_len, ROPE_DIM), F32), c32], axis=0)
    s32 = jnp.concatenate([jnp.zeros((ctx_len, ROPE_DIM), F32), s32], axis=0)
    n = c32.shape[0]
    cd = jnp.tile(c32, (1, LANES // ROPE_DIM))
    sd = jnp.tile(s32, (1, LANES // ROPE_DIM))
    pad = jnp.zeros((n, LANES - MLA_NOPE - MLA_ROPE), F32)
    cq = jnp.concatenate([jnp.ones((n, MLA_NOPE), F32), c32, pad], axis=-1)
    sq = jnp.concatenate([jnp.zeros((n, MLA_NOPE), F32), s32, pad], axis=-1)
    return cd, sd, cq, sq


def kernel(x, c, ctx, c_ctx, w_ada, b_ada, g_norm1, g_norm2, w_in, g_q_norm, w_uq, g_kv_norm, w_ukv,
           diff_lambda, g_diff_norm, hgrn_lower_bounds, g_hgrn_norm, w_out, w_ffn_gate, w_ffn_up,
           w_ffn_down, g_final):
    bsz, seq, d = x.shape
    ctx_len = ctx.shape[1]
    depth = w_in.shape[0]
    assert ctx_len % TOKEN_TILE == 0 and seq % TOKEN_TILE == 0 and seq % GRID_W == 0
    ctx_tiles = ctx_len // TOKEN_TILE

    cvec = jnp.concatenate([c, c_ctx[None, :], jnp.zeros((8 - bsz - 1, d), F32)], axis=0)
    mods = _adaln(cvec.T, bsz + 1, w_ada, b_ada)
    win, wuq, wukv = _layout_weights(w_in, w_uq, w_ukv)
    wo, wg, wu, wd = (w.astype(BF16) for w in (w_out, w_ffn_gate, w_ffn_up, w_ffn_down))
    tables = _rope_tables(seq, ctx_len)

    p = jax.nn.softmax(hgrn_lower_bounds.astype(F32), axis=0)
    cum = jnp.cumsum(p, axis=0)
    lb = (cum - cum[0:1]).reshape(depth, 1, 2 * HGRN_WIDTH)
    llb, l1mlb = jnp.log(lb), jnp.log1p(-lb)

    idx = np.arange(HGRN_WIDTH)
    seg = jnp.asarray(idx[:, None] // HGRN_DV == idx[None, :] // HGRN_DV, BF16)

    xall = jnp.concatenate([ctx, x], axis=1)
    for l in range(depth):
        lam_init = 0.8 - 0.6 * math.exp(-0.3 * l)
        mods3 = mods[l, 0:bsz + 1].reshape(bsz + 1, 1, 6 * d)
        qm, km, vtm, dq, dk, vtd, hq, hv, lf, hg = _pre_mixer(
            xall, mods3, ctx_tiles, l, g_norm1[l][None], win, g_q_norm[l][None], wuq, g_kv_norm[l][None], wukv,
            llb[l], l1mlb[l], tables)
        mla = _mla_attention(qm, km, vtm, ctx_tiles)
        g_col = jnp.tile(g_diff_norm[l], 2)[:, None]
        diff = _diff_attention(dq, dk, vtd, diff_lambda[l].astype(F32), g_col, ctx_tiles, lam_init)
        o_f = _hgrn_scan(hq, hv, lf, False, ctx_tiles)
        o_b = _hgrn_scan(hq, hv, lf, True, ctx_tiles)
        gh = jnp.tile(g_hgrn_norm[l], HGRN_HEADS)[None]
        xall = _post_mixer(xall, mods3, ctx_tiles, l, mla, diff, o_f, o_b, hg, gh, seg, wo, g_norm2[l][None],
                           wg, wu, wd)
    return _final_norm(xall, g_final[None], ctx_tiles, seq)
```

```python
import functools
import math

import jax
import jax.numpy as jnp
import numpy as np
from jax import lax
from jax.experimental import pallas as pl
from jax.experimental.pallas import tpu as pltpu

F32 = jnp.float32
BF16 = jnp.bfloat16

GRID_W = 64
EPS = 1e-6
ROPE_BASE = 10000.0
ROPE_DIM = 32

MLA_HEADS = 4
MLA_Q_RANK = 256
MLA_KV_RANK = 128
MLA_NOPE = 64
MLA_ROPE = ROPE_DIM
MLA_V = 64
MLA_SCALE = (MLA_NOPE + MLA_ROPE) ** -0.5

DIFF_HEADS = 4
DIFF_DK = ROPE_DIM
DIFF_DV = 2 * DIFF_DK
DIFF_SCALE = DIFF_DK ** -0.5
LOG2_E = math.log2(math.e)

HGRN_HEADS = 8
HGRN_DK = 64
HGRN_DV = 64
HGRN_WIDTH = HGRN_HEADS * HGRN_DV

LANES = 128
BF16_SUBLANES = 16
MXU_DIM = 256
VMEM_LIMIT_BYTES = 56 * 1024 * 1024

TOKEN_TILE = MXU_DIM
SCAN_CHUNK = 64

_IN_GROUPS = (
    ('cq', MLA_Q_RANK), ('ckv', MLA_KV_RANK), ('kr', LANES), ('kr_sw', LANES),
    ('dq', 256), ('dq_sw', 256), ('dk', 256), ('dk_sw', 256), ('dv', 256),
    ('hq', HGRN_WIDTH), ('hf', 2 * HGRN_WIDTH), ('hi', HGRN_WIDTH), ('hg', HGRN_WIDTH),
)
_IN_OFF = {}
_o = 0
for _n, _w in _IN_GROUPS:
    _IN_OFF[_n] = (_o, _o + _w)
    _o += _w
IN_WIDTH_PADDED = _o

_ROPE_SWAP = np.concatenate([np.arange(8, 16), np.arange(0, 8), np.arange(24, 32), np.arange(16, 24)])


def _silu(x):
    return x * jax.nn.sigmoid(x)


def _rms(x, g):
    return x * lax.rsqrt(jnp.mean(x * x, axis=-1, keepdims=True) + EPS) * g


def _dot(a, b):
    return jnp.dot(a, b, preferred_element_type=F32)


def _dot_nt(a, b):
    return lax.dot_general(a, b, (((1,), (1,)), ((), ())), preferred_element_type=F32)


def _params(n_grid):
    return pltpu.CompilerParams(dimension_semantics=('arbitrary',) * n_grid,
                                vmem_limit_bytes=VMEM_LIMIT_BYTES)


def _const_spec(shape):
    nd = len(shape)
    return pl.BlockSpec(shape, lambda *_: (0,) * nd, pipeline_mode=pl.Buffered(1))


def _layer_spec(stacked, layer):
    tail = stacked.shape[1:]
    return pl.BlockSpec((None,) + tail, lambda *_: (layer,) + (0,) * len(tail), pipeline_mode=pl.Buffered(1))


def _adaln_kernel(n_rows, cvt_ref, w_ref, b_ref, o_ref):
    s = _silu(cvt_ref[...])
    w = w_ref[0]
    outs = [jnp.sum(s[:, r:r + 1] * w, axis=0, keepdims=True) + b_ref[0] for r in range(n_rows)]
    outs.append(jnp.zeros((o_ref.shape[1] - n_rows, w.shape[1]), F32))
    o_ref[0] = jnp.concatenate(outs, axis=0)


def _adaln(cvec_t, n_rows, w_ada, b_ada):
    depth, d, n = w_ada.shape
    tn = 1536
    return pl.pallas_call(
        functools.partial(_adaln_kernel, n_rows),
        grid=(depth, n // tn),
        in_specs=[pl.BlockSpec((d, 8), lambda l, j: (0, 0)),
                  pl.BlockSpec((1, d, tn), lambda l, j: (l, 0, j)),
                  pl.BlockSpec((1, 1, tn), lambda l, j: (l, 0, j))],
        out_specs=pl.BlockSpec((1, 8, tn), lambda l, j: (l, 0, j)),
        out_shape=jax.ShapeDtypeStruct((depth, 8, n), F32),
        compiler_params=_params(2),
        name='adaln',
    )(cvec_t, w_ada, b_ada.reshape(depth, 1, n))


def _pre_kernel(x_ref, mod_ref, g1_ref, win_ref, gq_ref, wuq_ref, gkv_ref, wukv_ref, llb_ref, l1mlb_ref,
                cd_ref, sd_ref, cq_ref, sq_ref,
                qm_ref, km_ref, vtm_ref, dq_ref, dk_ref, vtd_ref, hq_ref, hv_ref, lf_ref, hg_ref):
    d = x_ref.shape[-1]
    x = x_ref[0]
    mod = mod_ref[0]
    sh1, sc1 = mod[:, 0:d], mod[:, d:2 * d]
    hb = (_rms(x, g1_ref[...]) * (1 + sc1) + sh1).astype(BF16)

    proj_all = _dot(hb, win_ref[...])

    def proj(name):
        lo, hi = _IN_OFF[name]
        return proj_all[:, lo:hi]

    cq_t, sq_t = cq_ref[...], sq_ref[...]
    cqn = _rms(proj('cq'), gq_ref[...]).astype(BF16)
    uq = _dot(cqn, wuq_ref[...])
    for h in range(MLA_HEADS):
        a = uq[:, h * LANES:(h + 1) * LANES]
        b = uq[:, (MLA_HEADS + h) * LANES:(MLA_HEADS + h + 1) * LANES]
        qm_ref[0, h] = (a * cq_t + b * sq_t).astype(BF16)
    ckvn = _rms(proj('ckv'), gkv_ref[...]).astype(BF16)
    ukv = _dot(ckvn, wukv_ref[...])
    krt = proj('kr') * cq_t + proj('kr_sw') * sq_t
    for h in range(MLA_HEADS):
        km_ref[0, h] = (ukv[:, h * LANES:(h + 1) * LANES] + krt).astype(BF16)
    for p in range(MLA_HEADS // 2):
        lo = (MLA_HEADS + p) * LANES
        vtm_ref[0, p, 0] = ukv[:, lo:lo + LANES].T.astype(BF16)
    cd2 = jnp.concatenate([cd_ref[...]] * 2, axis=1)
    sd2 = jnp.concatenate([sd_ref[...]] * 2, axis=1)
    dq_ref[0] = (proj('dq') * cd2 + proj('dq_sw') * sd2).astype(BF16)
    dk_ref[0] = (proj('dk') * cd2 + proj('dk_sw') * sd2).astype(BF16)
    dv = proj('dv')
    for p in range(DIFF_HEADS // 2):
        vtd_ref[0, p, 0] = dv[:, p * LANES:(p + 1) * LANES].T.astype(BF16)
    hq_ref[0] = _silu(proj('hq'))
    z = proj('hf')
    ls = jnp.minimum(z, 0.0) - jnp.log1p(jnp.exp(-jnp.abs(z)))
    a = llb_ref[...]
    b = l1mlb_ref[...] + ls
    lf_ref[0] = jnp.maximum(a, b) + jnp.log1p(jnp.exp(-jnp.abs(a - b)))
    hv_ref[0] = proj('hi').astype(BF16)
    hg_ref[0] = proj('hg')


def _mod_spec(mods3, ctx_tiles):
    ctx_row = mods3.shape[0] - 1
    return pl.BlockSpec((1, 1, mods3.shape[-1]), lambda b, i: (jnp.where(i < ctx_tiles, ctx_row, b), 0, 0))


def _pre_mixer(xall, mods3, ctx_tiles, layer, g1, win, gq, wuq, gkv, wukv, llb, l1mlb, tables):
    bsz, s, d = xall.shape
    tm = TOKEN_TILE
    ns = s // tm
    grid = (bsz, ns)
    tok = lambda w: pl.BlockSpec((1, tm, w), lambda b, i: (b, i, 0))
    tab = pl.BlockSpec((tm, LANES), lambda b, i: (i, 0))
    head4 = pl.BlockSpec((1, MLA_HEADS, tm, LANES), lambda b, i: (b, 0, i, 0))
    vt = pl.BlockSpec((1, 2, 1, LANES, tm), lambda b, i: (b, 0, i, 0, 0))
    in_specs = [
        tok(d),
        _mod_spec(mods3, ctx_tiles),
        _const_spec(g1.shape), _layer_spec(win, layer), _const_spec(gq.shape), _layer_spec(wuq, layer),
        _const_spec(gkv.shape), _layer_spec(wukv, layer), _const_spec(llb.shape), _const_spec(l1mlb.shape),
        tab, tab, tab, tab,
    ]
    sds = jax.ShapeDtypeStruct
    out_shape = (
        sds((bsz, MLA_HEADS, s, LANES), BF16), sds((bsz, MLA_HEADS, s, LANES), BF16),
        sds((bsz, 2, ns, LANES, tm), BF16),
        sds((bsz, s, 256), BF16), sds((bsz, s, 256), BF16), sds((bsz, 2, ns, LANES, tm), BF16),
        sds((bsz, s, HGRN_WIDTH), F32), sds((bsz, s, HGRN_WIDTH), BF16),
        sds((bsz, s, 2 * HGRN_WIDTH), F32), sds((bsz, s, HGRN_WIDTH), F32),
    )
    out_specs = (head4, head4, vt, tok(256), tok(256), vt,
                 tok(HGRN_WIDTH), tok(HGRN_WIDTH), tok(2 * HGRN_WIDTH), tok(HGRN_WIDTH))
    return pl.pallas_call(
        _pre_kernel, grid=grid, in_specs=in_specs, out_specs=out_specs, out_shape=out_shape,
        compiler_params=_params(2), name='pre_mixer',
    )(xall, mods3, g1, win, gq, wuq, gkv, wukv, llb, l1mlb, *tables)


def _attend(streams, n_groups, tkg, dv, s_ref):
    tq = s_ref.shape[-1]

    ones_rows = (lax.broadcasted_iota(jnp.int32, (BF16_SUBLANES, tkg), 0) == 0).astype(BF16)

    def score(si, g, slot):
        q, key_rows, _ = streams[si]
        s = _dot_nt(key_rows(g), q)
        s_ref[si, slot, 0:tkg, :] = s
        return jnp.max(s, axis=0, keepdims=True)

    def softmax_pv(si, g, slot, state, mx):
        m, acc = state
        m_new = jnp.maximum(m, mx)
        alpha = jnp.exp2(m - m_new)
        p = jnp.exp2(s_ref[si, slot, 0:tkg, :] - m_new).astype(BF16)
        v1 = jnp.concatenate([streams[si][2](g), ones_rows], axis=0)
        return m_new, alpha * acc + _dot(v1, p)

    n = len(streams)
    carry = [(jnp.full((1, tq), -jnp.inf, F32), jnp.zeros((dv + BF16_SUBLANES, tq), F32)) for _ in streams]
    maxes = [score(si, 0, 0) for si in range(n)]
    for g in range(n_groups):
        slot, next_slot = g % 2, (g + 1) % 2
        has_next = g + 1 < n_groups
        next_maxes = [None] * n
        if has_next:
            next_maxes[0] = score(0, g + 1, next_slot)
        for si in range(n):
            if has_next and si + 1 < n:
                next_maxes[si + 1] = score(si + 1, g + 1, next_slot)
            carry[si] = softmax_pv(si, g, slot, carry[si], maxes[si])
        maxes = next_maxes
    return [(acc[0:dv], acc[dv:dv + 1]) for (_, acc) in carry]


def _chunks_per_group(n_chunks):
    return next(u for u in (3, 4, 2, 1) if n_chunks % u == 0)


def _for_context_and_latent(i, ctx_chunks, all_chunks, run):
    @pl.when(i < ctx_chunks)
    def _():
        u = _chunks_per_group(ctx_chunks)
        run(ctx_chunks // u, u)

    @pl.when(i >= ctx_chunks)
    def _():
        u = _chunks_per_group(all_chunks)
        run(all_chunks // u, u)


def _key_rows(k_ref, lead, tk, u):
    return lambda g: k_ref[lead + (pl.ds(pl.multiple_of(g * (u * tk), u * tk), u * tk), slice(None))]


def _value_t(vt_ref, u, hh, dv, pair=0):
    def get(g):
        parts = [vt_ref[0, pair, g * u + c, hh * dv:(hh + 1) * dv, :] for c in range(u)]
        return parts[0] if u == 1 else jnp.concatenate(parts, axis=1)
    return get


def _score_scratch(n_streams, ctx_chunks, all_chunks, tk, tq):
    rows = max(_chunks_per_group(ctx_chunks), _chunks_per_group(all_chunks)) * tk
    return pltpu.VMEM((n_streams, 2, rows, tq), F32)


def _mla_kernel(ctx_chunks, q_ref, k_ref, vt_ref, o_ref, s_ref):
    tq = q_ref.shape[2]
    tk = vt_ref.shape[-1]

    def run(n_groups, u):
        streams = [(q_ref[0, h], _key_rows(k_ref, (0, h), tk, u), _value_t(vt_ref, u, h % 2, MLA_V, h // 2))
                   for h in range(MLA_HEADS)]
        res = _attend(streams, n_groups, u * tk, MLA_V, s_ref)
        ot = jnp.concatenate([acc / l for acc, l in res], axis=0)
        o_ref[0] = ot.T.astype(o_ref.dtype)

    _for_context_and_latent(pl.program_id(1), ctx_chunks, vt_ref.shape[2], run)


def _mla_attention(qm, km, vtm, ctx_chunks):
    bsz, _, s, _ = qm.shape
    nc, tk = vtm.shape[2], vtm.shape[4]
    tq = TOKEN_TILE
    return pl.pallas_call(
        functools.partial(_mla_kernel, ctx_chunks),
        grid=(bsz, s // tq),
        in_specs=[pl.BlockSpec((1, MLA_HEADS, tq, LANES), lambda b, i: (b, 0, i, 0)),
                  pl.BlockSpec((1, MLA_HEADS, s, LANES), lambda b, i: (b, 0, 0, 0)),
                  pl.BlockSpec((1, 2, nc, LANES, tk), lambda b, i: (b, 0, 0, 0, 0))],
        out_specs=pl.BlockSpec((1, tq, MLA_HEADS * MLA_V), lambda b, i: (b, i, 0)),
        out_shape=jax.ShapeDtypeStruct((bsz, s, MLA_HEADS * MLA_V), BF16),
        scratch_shapes=[_score_scratch(MLA_HEADS, ctx_chunks, nc, tk, tq)],
        compiler_params=_params(2), name='mla_attention',
    )(qm, km, vtm)


def _diff_kernel(ctx_chunks, lam_init, q_ref, k_ref, vt_ref, lam_ref, g_ref, o_ref, s_ref):
    tq = q_ref.shape[1]
    tk = vt_ref.shape[-1]
    pair = pl.program_id(1)

    def run(n_groups, u):
        lp = lam_ref[...]
        lam = (jnp.exp(jnp.sum(lp[0:1] * lp[1:2], axis=1, keepdims=True))
               - jnp.exp(jnp.sum(lp[2:3] * lp[3:4], axis=1, keepdims=True)) + lam_init)
        q = q_ref[0]
        lane = lax.broadcasted_iota(jnp.int32, q.shape, 1)
        key_rows = _key_rows(k_ref, (0,), tk, u)
        streams = []
        for hh in range(2):
            for mp in range(2):
                base = ((2 * pair + hh) * 2 + mp) * DIFF_DK
                qsel = jnp.where((lane >= base) & (lane < base + DIFF_DK), q, jnp.zeros_like(q))
                streams.append((qsel, key_rows, _value_t(vt_ref, u, hh, DIFF_DV)))
        res = _attend(streams, n_groups, u * tk, DIFF_DV, s_ref)
        outs = []
        for hh in range(2):
            (a1, l1), (a2, l2) = res[2 * hh], res[2 * hh + 1]
            outs.append(a1 / l1 - lam * (a2 / l2))
        normed = [o * lax.rsqrt(jnp.mean(o * o, axis=0, keepdims=True) + EPS) for o in outs]
        y = jnp.concatenate(normed, axis=0) * g_ref[...] * (1 - lam_init)
        o_ref[0] = y.T.astype(o_ref.dtype)

    _for_context_and_latent(pl.program_id(2), ctx_chunks, vt_ref.shape[2], run)


def _diff_attention(dq, dk, vtd, lam_params, g_col, ctx_chunks, lam_init):
    bsz, s, w = dq.shape
    nc, tk = vtd.shape[2], vtd.shape[4]
    tq = TOKEN_TILE
    return pl.pallas_call(
        functools.partial(_diff_kernel, ctx_chunks, lam_init),
        grid=(bsz, DIFF_HEADS // 2, s // tq),
        in_specs=[pl.BlockSpec((1, tq, w), lambda b, p, i: (b, i, 0)),
                  pl.BlockSpec((1, s, w), lambda b, p, i: (b, 0, 0)),
                  pl.BlockSpec((1, 1, nc, LANES, tk), lambda b, p, i: (b, p, 0, 0, 0)),
                  pl.BlockSpec(lam_params.shape, lambda b, p, i: (0, 0)),
                  pl.BlockSpec(g_col.shape, lambda b, p, i: (0, 0))],
        out_specs=pl.BlockSpec((1, tq, LANES), lambda b, p, i: (b, i, p)),
        out_shape=jax.ShapeDtypeStruct((bsz, s, DIFF_HEADS * DIFF_DV), BF16),
        scratch_shapes=[_score_scratch(4, ctx_chunks, nc, tk, tq)],
        compiler_params=_params(3), name='diff_attention',
    )(dq, dk, vtd, lam_params, g_col)


def _hgrn_kernel(reverse, q_ref, v_ref, lf_ref, o_ref, st_ref):
    c = SCAN_CHUNK
    tile, width = q_ref.shape[1], q_ref.shape[2]
    n_ch, pairs = tile // c, width // LANES
    log_c = c.bit_length() - 1
    tok = lax.broadcasted_iota(jnp.int32, (c, 2 * c), 0)
    src = lax.broadcasted_iota(jnp.int32, (c, 2 * c), 1) & (c - 1)
    pos_t = (c - 1 - tok) if reverse else tok
    pos_s = (c - 1 - src) if reverse else src
    tok_col = lax.broadcasted_iota(jnp.int32, (tile, 1), 0) & (c - 1)
    pos_col = (c - 1 - tok_col) if reverse else tok_col
    head_a_all = (lax.broadcasted_iota(jnp.int32, (tile, width), 1) & (LANES - 1)) < HGRN_DK
    r2 = lax.broadcasted_iota(jnp.int32, (LANES, LANES), 0)
    c2 = lax.broadcasted_iota(jnp.int32, (LANES, LANES), 1)
    same_head = (r2 < HGRN_DV) == (c2 < HGRN_DK)
    tok2 = lax.broadcasted_iota(jnp.int32, (tile, tile), 0)
    src2 = lax.broadcasted_iota(jnp.int32, (tile, tile), 1)
    tri = (((tok2 >> log_c) == (src2 >> log_c)) & ((src2 >= tok2) if reverse else (src2 <= tok2))).astype(BF16)
    earlier = lambda x, n: pltpu.roll(x, (tile - n) if reverse else n, 0)
    later_by = lambda x, n: pltpu.roll(x, n if reverse else (tile - n), 0)

    @pl.when(pl.program_id(1) == 0)
    def _():
        st_ref[...] = jnp.zeros_like(st_ref)

    rows = lambda ci: slice(ci * c, (ci + 1) * c)
    lanes = lambda p: slice(p * LANES, (p + 1) * LANES)
    cells = [(ci, p) for ci in range(n_ch) for p in range(pairs)]

    q, v, lf = q_ref[0], v_ref[0], lf_ref[0]
    p0 = lf.astype(BF16)
    r0 = lf - p0.astype(F32)
    p1 = r0.astype(BF16)
    p2 = (r0 - p1.astype(F32)).astype(BF16)
    cum = (_dot(tri, p0) + _dot(tri, p1) + _dot(tri, p2)) * LOG2_E
    k = 1.0 - jnp.exp(lf)
    zero = jnp.zeros((), BF16)
    q16 = q.astype(BF16)
    k16 = k.astype(BF16)
    ka, kb = jnp.where(head_a_all, k16, zero), jnp.where(head_a_all, zero, k16)
    va, vb = jnp.where(head_a_all, v, zero), jnp.where(head_a_all, zero, v)

    scores = {cell: None for cell in cells}

    def add_stage(qd, kda, kdb, keep):
        for ci, p in cells:
            kd2 = jnp.concatenate([kda[rows(ci), lanes(p)], kdb[rows(ci), lanes(p)]], axis=0)
            sc = jnp.where(keep, _dot_nt(qd[rows(ci), lanes(p)], kd2), 0.0)
            scores[ci, p] = sc if scores[ci, p] is None else scores[ci, p] + sc

    end = cum
    block = 2
    while block <= c:
        half = block // 2
        first = (pos_col & (block - 1)) < half
        ref = jnp.where(first, end, earlier(end, half))
        end = jnp.where(first, later_by(end, half), end)
        e = jnp.exp2(-jnp.abs(cum - ref)).astype(BF16)
        e_q = jnp.where(first, zero, e)
        e_k = jnp.where(first, e, zero)
        shift = block.bit_length() - 1
        add_stage(q16 * e_q, ka * e_k, kb * e_k, (pos_t >> shift) == (pos_s >> shift))
        block *= 2
    add_stage(q16, ka, kb, tok == src)
    tot = end
    qs = (q * jnp.exp2(cum)).astype(BF16)
    ks = (k * jnp.exp2(tot - cum)).astype(BF16)

    intra, upd = {}, {}
    for ci, p in cells:
        v2 = jnp.concatenate([va[rows(ci), lanes(p)], vb[rows(ci), lanes(p)]], axis=0)
        intra[ci, p] = _dot(scores[ci, p].astype(BF16), v2)
        vt = v[rows(ci), lanes(p)].astype(F32).T.astype(BF16)
        upd[ci, p] = jnp.where(same_head, _dot(vt, ks[rows(ci), lanes(p)]), 0.0)

    states = [st_ref[p] for p in range(pairs)]
    for ci in (range(n_ch - 1, -1, -1) if reverse else range(n_ch)):
        decay = jnp.exp2(tot[ci * c:ci * c + 1, :])
        outs = []
        for p in range(pairs):
            outs.append(_dot_nt(qs[rows(ci), lanes(p)], states[p].astype(BF16)) + intra[ci, p])
            states[p] = states[p] * decay[:, lanes(p)] + upd[ci, p]
        o_ref[0, rows(ci), :] = jnp.concatenate(outs, axis=1)
    for p in range(pairs):
        st_ref[p] = states[p]


def _hgrn_scan(hq, hv, lf, reverse, ctx_tiles):
    bsz, s, w = hq.shape
    tile = TOKEN_TILE
    ns = s // tile
    if reverse:
        blk = lambda i: jnp.where(i < ctx_tiles, ctx_tiles - 1 - i, ns - 1 + ctx_tiles - i)
    else:
        blk = lambda i: i
    spec = lambda lane_block: pl.BlockSpec((1, tile, w), lambda b, i: (b, blk(i), lane_block))
    return pl.pallas_call(
        functools.partial(_hgrn_kernel, reverse),
        grid=(bsz, ns),
        in_specs=[spec(0), spec(0), spec(1 if reverse else 0)],
        out_specs=spec(0),
        out_shape=jax.ShapeDtypeStruct((bsz, s, w), F32),
        scratch_shapes=[pltpu.VMEM((w // LANES, LANES, LANES), F32)],
        compiler_params=_params(2), name='hgrn_bwd' if reverse else 'hgrn_fwd',
    )(hq, hv, lf)


def _post_kernel(ffn_chunk, x_ref, mod_ref, mla_ref, diff_ref, of_ref, ob_ref, hg_ref, gh_ref, seg_ref,
                 wo_ref, g2_ref, wg_ref, wu_ref, wd_ref, o_ref):
    d = x_ref.shape[-1]
    x = x_ref[0]
    mod = mod_ref[0]
    gt1, sh2, sc2, gt2 = (mod[:, 2 * d:3 * d], mod[:, 3 * d:4 * d], mod[:, 4 * d:5 * d], mod[:, 5 * d:6 * d])
    o = of_ref[0] + ob_ref[0]
    sq = o * o
    sq_hi = sq.astype(BF16)
    sq_lo = (sq - sq_hi.astype(F32)).astype(BF16)
    seg = seg_ref[...]
    ms = (_dot(sq_hi, seg) + _dot(sq_lo, seg)) * (1.0 / HGRN_DV)
    hn = (o * lax.rsqrt(ms + EPS) * gh_ref[...] * _silu(hg_ref[0])).astype(BF16)
    na, nb = mla_ref.shape[-1], diff_ref.shape[-1]
    y = (_dot(mla_ref[0], wo_ref[0:na, :]) + _dot(diff_ref[0], wo_ref[na:na + nb, :])
         + _dot(hn, wo_ref[na + nb:, :]))
    x1 = x + gt1 * y
    h2 = (_rms(x1, g2_ref[...]) * (1 + sc2) + sh2).astype(BF16)
    dff = wg_ref.shape[1]
    acc = jnp.zeros_like(x1)
    for lo in range(0, dff, ffn_chunk):
        g = _dot(h2, wg_ref[:, lo:lo + ffn_chunk])
        u = _dot(h2, wu_ref[:, lo:lo + ffn_chunk])
        acc = acc + _dot((_silu(g) * u).astype(BF16), wd_ref[lo:lo + ffn_chunk, :])
    o_ref[0] = x1 + gt2 * acc


def _post_mixer(xall, mods3, ctx_tiles, layer, mla, diff, o_f, o_b, hg, gh, seg, wo, g2, wg, wu, wd):
    bsz, s, d = xall.shape
    tm = TOKEN_TILE
    ffn_chunk = wg.shape[-1] // 2
    tok = lambda w: pl.BlockSpec((1, tm, w), lambda b, i: (b, i, 0))
    in_specs = [
        tok(d),
        _mod_spec(mods3, ctx_tiles),
        tok(mla.shape[-1]), tok(diff.shape[-1]), tok(HGRN_WIDTH), tok(HGRN_WIDTH), tok(HGRN_WIDTH),
        _const_spec(gh.shape), _const_spec(seg.shape), _layer_spec(wo, layer), _const_spec(g2.shape),
        _layer_spec(wg, layer), _layer_spec(wu, layer), _layer_spec(wd, layer),
    ]
    return pl.pallas_call(
        functools.partial(_post_kernel, ffn_chunk),
        grid=(bsz, s // tm), in_specs=in_specs, out_specs=tok(d),
        out_shape=jax.ShapeDtypeStruct((bsz, s, d), F32),
        compiler_params=_params(2), name='post_mixer',
    )(xall, mods3, mla, diff, o_f, o_b, hg, gh, seg, wo, g2, wg, wu, wd)


def _final_kernel(x_ref, g_ref, o_ref):
    o_ref[0] = _rms(x_ref[0], g_ref[...])


def _final_norm(xall, g, ctx_tiles, seq):
    bsz, _, d = xall.shape
    tm = TOKEN_TILE
    return pl.pallas_call(
        _final_kernel,
        grid=(bsz, seq // tm),
        in_specs=[pl.BlockSpec((1, tm, d), lambda b, i: (b, i + ctx_tiles, 0)),
                  pl.BlockSpec(g.shape, lambda b, i: (0, 0))],
        out_specs=pl.BlockSpec((1, tm, d), lambda b, i: (b, i, 0)),
        out_shape=jax.ShapeDtypeStruct((bsz, seq, d), F32),
        compiler_params=_params(2), name='final_norm',
    )(xall, g)


def _split_cols(w, sizes):
    return jnp.split(w, np.cumsum(sizes)[:-1].tolist(), axis=-1)


def _swap32(w):
    lead = w.shape[:-1]
    return w.reshape(*lead, -1, ROPE_DIM)[..., _ROPE_SWAP].reshape(*lead, -1)


def _layout_weights(w_in, w_uq, w_ukv):
    depth, d, _ = w_in.shape
    in_sizes = (MLA_Q_RANK, MLA_KV_RANK, MLA_ROPE, 256, 256, 256,
                HGRN_WIDTH, HGRN_WIDTH, HGRN_WIDTH, HGRN_WIDTH, HGRN_WIDTH)
    cq, ckv, kr, dq, dk, dv, hq, hff, hfb, hi, hg = _split_cols(w_in, in_sizes)
    zeros = lambda rows, n: jnp.zeros((depth, rows, n), F32)
    rope_tile = lambda w: jnp.concatenate([zeros(w.shape[1], MLA_NOPE), w, zeros(w.shape[1], LANES - MLA_NOPE - MLA_ROPE)], -1)
    dqs = dq * (DIFF_SCALE * LOG2_E)
    win = jnp.concatenate([cq, ckv, rope_tile(kr), rope_tile(_swap32(kr)), dqs, _swap32(dqs), dk, _swap32(dk), dv,
                           hq, hff, hfb, hi, hg], axis=-1).astype(BF16)
    per = MLA_NOPE + MLA_ROPE
    uq_tiles, uq_sw_tiles = [], []
    for h in range(MLA_HEADS):
        nope = w_uq[..., h * per:h * per + MLA_NOPE]
        rope = w_uq[..., h * per + MLA_NOPE:(h + 1) * per]
        uq_tiles.append(jnp.concatenate([nope, rope, zeros(MLA_Q_RANK, LANES - per)], -1))
        uq_sw_tiles.append(rope_tile(_swap32(rope)))
    wuq = (jnp.concatenate(uq_tiles + uq_sw_tiles, axis=-1) * (MLA_SCALE * LOG2_E)).astype(BF16)
    kv = MLA_NOPE + MLA_V
    k_tiles = [jnp.concatenate([w_ukv[..., h * kv:h * kv + MLA_NOPE], zeros(MLA_KV_RANK, LANES - MLA_NOPE)], -1)
               for h in range(MLA_HEADS)]
    v_cols = [w_ukv[..., h * kv + MLA_NOPE:(h + 1) * kv] for h in range(MLA_HEADS)]
    wukv = jnp.concatenate(k_tiles + v_cols, axis=-1).astype(BF16)
    return win, wuq, wukv


def _rope_tables(seq, ctx_len):
    t = jnp.arange(seq)
    row = (t // GRID_W).astype(F32)
    col = (t % GRID_W).astype(F32)
    n_freq = ROPE_DIM // 4
    freqs = ROPE_BASE ** (-jnp.arange(n_freq, dtype=F32) / n_freq)
    ang_r = row[:, None] * freqs
    ang_c = col[:, None] * freqs
    cr, sr, cc, sc = jnp.cos(ang_r), jnp.sin(ang_r), jnp.cos(ang_c), jnp.sin(ang_c)
    c32 = jnp.concatenate([cr, cr, cc, cc], axis=-1)
    s32 = jnp.concatenate([-sr, sr, -sc, sc], axis=-1)
    c32 = jnp.concatenate([jnp.ones((ctx_len, ROPE_DIM), F32), c32], axis=0)
    s32 = jnp.concatenate([jnp.zeros((ctx_len, ROPE_DIM), F32), s32], axis=0)
    n = c32.shape[0]
    cd = jnp.tile(c32, (1, LANES // ROPE_DIM))
    sd = jnp.tile(s32, (1, LANES // ROPE_DIM))
    pad = jnp.zeros((n, LANES - MLA_NOPE - MLA_ROPE), F32)
    cq = jnp.concatenate([jnp.ones((n, MLA_NOPE), F32), c32, pad], axis=-1)
    sq = jnp.concatenate([jnp.zeros((n, MLA_NOPE), F32), s32, pad], axis=-1)
    return cd, sd, cq, sq


def kernel(x, c, ctx, c_ctx, w_ada, b_ada, g_norm1, g_norm2, w_in, g_q_norm, w_uq, g_kv_norm, w_ukv,
           diff_lambda, g_diff_norm, hgrn_lower_bounds, g_hgrn_norm, w_out, w_ffn_gate, w_ffn_up,
           w_ffn_down, g_final):
    bsz, seq, d = x.shape
    ctx_len = ctx.shape[1]
    depth = w_in.shape[0]
    assert ctx_len % TOKEN_TILE == 0 and seq % TOKEN_TILE == 0 and seq % GRID_W == 0
    ctx_tiles = ctx_len // TOKEN_TILE

    cvec = jnp.concatenate([c, c_ctx[None, :], jnp.zeros((8 - bsz - 1, d), F32)], axis=0)
    mods = _adaln(cvec.T, bsz + 1, w_ada, b_ada)
    win, wuq, wukv = _layout_weights(w_in, w_uq, w_ukv)
    wo, wg, wu, wd = (w.astype(BF16) for w in (w_out, w_ffn_gate, w_ffn_up, w_ffn_down))
    tables = _rope_tables(seq, ctx_len)

    p = jax.nn.softmax(hgrn_lower_bounds.astype(F32), axis=0)
    cum = jnp.cumsum(p, axis=0)
    lb = (cum - cum[0:1]).reshape(depth, 1, 2 * HGRN_WIDTH)
    llb, l1mlb = jnp.log(lb), jnp.log1p(-lb)

    idx = np.arange(HGRN_WIDTH)
    seg = jnp.asarray(idx[:, None] // HGRN_DV == idx[None, :] // HGRN_DV, BF16)

    xall = jnp.concatenate([ctx, x], axis=1)
    for l in range(depth):
        lam_init = 0.8 - 0.6 * math.exp(-0.3 * l)
        mods3 = mods[l, 0:bsz + 1].reshape(bsz + 1, 1, 6 * d)
        qm, km, vtm, dq, dk, vtd, hq, hv, lf, hg = _pre_mixer(
            xall, mods3, ctx_tiles, l, g_norm1[l][None], win, g_q_norm[l][None], wuq, g_kv_norm[l][None], wukv,
            llb[l], l1mlb[l], tables)
        mla = _mla_attention(qm, km, vtm, ctx_tiles)
        g_col = jnp.tile(g_diff_norm[l], 2)[:, None]
        diff = _diff_attention(dq, dk, vtd, diff_lambda[l].astype(F32), g_col, ctx_tiles, lam_init)
        o_f = _hgrn_scan(hq, hv, lf, False, ctx_tiles)
        o_b = _hgrn_scan(hq, hv, lf, True, ctx_tiles)
        gh = jnp.tile(g_hgrn_norm[l], HGRN_HEADS)[None]
        xall = _post_mixer(xall, mods3, ctx_tiles, l, mla, diff, o_f, o_b, hg, gh, seg, wo, g_norm2[l][None],
                           wg, wu, wd)
    return _final_norm(xall, g_final[None], ctx_tiles, seq)
```

```python
import functools
import math

import jax
import jax.numpy as jnp
import numpy as np
from jax import lax
from jax.experimental import pallas as pl
from jax.experimental.pallas import tpu as pltpu

F32 = jnp.float32
BF16 = jnp.bfloat16

GRID_W = 64
EPS = 1e-6
ROPE_BASE = 10000.0
ROPE_DIM = 32

MLA_HEADS = 4
MLA_Q_RANK = 256
MLA_KV_RANK = 128
MLA_NOPE = 64
MLA_ROPE = ROPE_DIM
MLA_V = 64
MLA_SCALE = (MLA_NOPE + MLA_ROPE) ** -0.5

DIFF_HEADS = 4
DIFF_DK = ROPE_DIM
DIFF_DV = 2 * DIFF_DK
DIFF_SCALE = DIFF_DK ** -0.5
LOG2_E = math.log2(math.e)

HGRN_HEADS = 8
HGRN_DK = 64
HGRN_DV = 64
HGRN_WIDTH = HGRN_HEADS * HGRN_DV
DIFF_QK_WIDTH = DIFF_HEADS * 2 * DIFF_DK
DIFF_WIDTH = DIFF_HEADS * DIFF_DV

LANES = 128
SUBLANES = 8
BF16_SUBLANES = 16
MXU_DIM = 256
VMEM_CAPACITY_BYTES = 64 * 1024 * 1024
VMEM_LIMIT_BYTES = VMEM_CAPACITY_BYTES - VMEM_CAPACITY_BYTES // 8

TOKEN_TILE = MXU_DIM
SCAN_CHUNK = 64
ADALN_COL_TILE = 12 * LANES

_IN_GROUPS = (
    ('cq', MLA_Q_RANK), ('ckv', MLA_KV_RANK), ('kr', LANES), ('kr_sw', LANES),
    ('dq', DIFF_QK_WIDTH), ('dq_sw', DIFF_QK_WIDTH), ('dk', DIFF_QK_WIDTH), ('dk_sw', DIFF_QK_WIDTH),
    ('dv', DIFF_WIDTH),
    ('hq', HGRN_WIDTH), ('hf', 2 * HGRN_WIDTH), ('hi', HGRN_WIDTH), ('hg', HGRN_WIDTH),
)
_IN_OFF = {}
_o = 0
for _n, _w in _IN_GROUPS:
    _IN_OFF[_n] = (_o, _o + _w)
    _o += _w
IN_WIDTH_PADDED = _o

_ROPE_SWAP = np.concatenate([np.arange(8, 16), np.arange(0, 8), np.arange(24, 32), np.arange(16, 24)])


def _silu(x):
    return x * jax.nn.sigmoid(x)


def _rms(x, g):
    return x * lax.rsqrt(jnp.mean(x * x, axis=-1, keepdims=True) + EPS) * g


def _dot(a, b):
    return jnp.dot(a, b, preferred_element_type=F32)


def _dot_nt(a, b):
    return lax.dot_general(a, b, (((1,), (1,)), ((), ())), preferred_element_type=F32)


def _params(n_grid):
    return pltpu.CompilerParams(dimension_semantics=('arbitrary',) * n_grid,
                                vmem_limit_bytes=VMEM_LIMIT_BYTES)


def _const_spec(shape):
    nd = len(shape)
    return pl.BlockSpec(shape, lambda *_: (0,) * nd, pipeline_mode=pl.Buffered(1))


def _layer_spec(stacked, layer):
    tail = stacked.shape[1:]
    return pl.BlockSpec((None,) + tail, lambda *_: (layer,) + (0,) * len(tail), pipeline_mode=pl.Buffered(1))


def _adaln_kernel(n_rows, cvt_ref, w_ref, b_ref, o_ref):
    s = _silu(cvt_ref[...])
    w = w_ref[0]
    outs = [jnp.sum(s[:, r:r + 1] * w, axis=0, keepdims=True) + b_ref[0] for r in range(n_rows)]
    outs.append(jnp.zeros((o_ref.shape[1] - n_rows, w.shape[1]), F32))
    o_ref[0] = jnp.concatenate(outs, axis=0)


def _adaln(cvec_t, n_rows, w_ada, b_ada):
    depth, d, n = w_ada.shape
    tn = ADALN_COL_TILE
    assert n % tn == 0 and n_rows <= SUBLANES
    return pl.pallas_call(
        functools.partial(_adaln_kernel, n_rows),
        grid=(depth, n // tn),
        in_specs=[pl.BlockSpec((d, SUBLANES), lambda l, j: (0, 0)),
                  pl.BlockSpec((1, d, tn), lambda l, j: (l, 0, j)),
                  pl.BlockSpec((1, 1, tn), lambda l, j: (l, 0, j))],
        out_specs=pl.BlockSpec((1, SUBLANES, tn), lambda l, j: (l, 0, j)),
        out_shape=jax.ShapeDtypeStruct((depth, SUBLANES, n), F32),
        compiler_params=_params(2),
        name='adaln',
    )(cvec_t, w_ada, b_ada.reshape(depth, 1, n))


def _pre_kernel(x_ref, mod_ref, g1_ref, win_ref, gq_ref, wuq_ref, gkv_ref, wukv_ref, llb_ref, l1mlb_ref,
                cd_ref, sd_ref, cq_ref, sq_ref,
                qm_ref, km_ref, vtm_ref, dq_ref, dk_ref, vtd_ref, hq_ref, hv_ref, lf_ref, hg_ref):
    d = x_ref.shape[-1]
    x = x_ref[0]
    mod = mod_ref[0]
    sh1, sc1 = mod[:, 0:d], mod[:, d:2 * d]
    hb = (_rms(x, g1_ref[...]) * (1 + sc1) + sh1).astype(BF16)

    proj_all = _dot(hb, win_ref[...])

    def proj(name):
        lo, hi = _IN_OFF[name]
        return proj_all[:, lo:hi]

    cq_t, sq_t = cq_ref[...], sq_ref[...]
    cqn = _rms(proj('cq'), gq_ref[...]).astype(BF16)
    uq = _dot(cqn, wuq_ref[...])
    for h in range(MLA_HEADS):
        a = uq[:, h * LANES:(h + 1) * LANES]
        b = uq[:, (MLA_HEADS + h) * LANES:(MLA_HEADS + h + 1) * LANES]
        qm_ref[0, h] = (a * cq_t + b * sq_t).astype(BF16)
    ckvn = _rms(proj('ckv'), gkv_ref[...]).astype(BF16)
    ukv = _dot(ckvn, wukv_ref[...])
    krt = proj('kr') * cq_t + proj('kr_sw') * sq_t
    for h in range(MLA_HEADS):
        km_ref[0, h] = (ukv[:, h * LANES:(h + 1) * LANES] + krt).astype(BF16)
    for p in range(MLA_HEADS // 2):
        lo = (MLA_HEADS + p) * LANES
        vtm_ref[0, p, 0] = ukv[:, lo:lo + LANES].T.astype(BF16)
    cd2 = jnp.concatenate([cd_ref[...]] * 2, axis=1)
    sd2 = jnp.concatenate([sd_ref[...]] * 2, axis=1)
    dq_ref[0] = (proj('dq') * cd2 + proj('dq_sw') * sd2).astype(BF16)
    dk_ref[0] = (proj('dk') * cd2 + proj('dk_sw') * sd2).astype(BF16)
    dv = proj('dv')
    for p in range(DIFF_HEADS // 2):
        vtd_ref[0, p, 0] = dv[:, p * LANES:(p + 1) * LANES].T.astype(BF16)
    hq_ref[0] = _silu(proj('hq'))
    z = proj('hf')
    ls = jnp.minimum(z, 0.0) - jnp.log1p(jnp.exp(-jnp.abs(z)))
    a = llb_ref[...]
    b = l1mlb_ref[...] + ls
    lf_ref[0] = jnp.maximum(a, b) + jnp.log1p(jnp.exp(-jnp.abs(a - b)))
    hv_ref[0] = proj('hi').astype(BF16)
    hg_ref[0] = proj('hg')


def _mod_spec(mods3, ctx_tiles):
    ctx_row = mods3.shape[0] - 1
    return pl.BlockSpec((1, 1, mods3.shape[-1]), lambda b, i: (jnp.where(i < ctx_tiles, ctx_row, b), 0, 0))


def _pre_mixer(xall, mods3, ctx_tiles, layer, g1, win, gq, wuq, gkv, wukv, llb, l1mlb, tables):
    bsz, s, d = xall.shape
    tm = TOKEN_TILE
    ns = s // tm
    grid = (bsz, ns)
    tok = lambda w: pl.BlockSpec((1, tm, w), lambda b, i: (b, i, 0))
    tab = pl.BlockSpec((tm, LANES), lambda b, i: (i, 0))
    head4 = pl.BlockSpec((1, MLA_HEADS, tm, LANES), lambda b, i: (b, 0, i, 0))
    vt = pl.BlockSpec((1, 2, 1, LANES, tm), lambda b, i: (b, 0, i, 0, 0))
    in_specs = [
        tok(d),
        _mod_spec(mods3, ctx_tiles),
        _const_spec(g1.shape), _layer_spec(win, layer), _const_spec(gq.shape), _layer_spec(wuq, layer),
        _const_spec(gkv.shape), _layer_spec(wukv, layer), _const_spec(llb.shape), _const_spec(l1mlb.shape),
        tab, tab, tab, tab,
    ]
    sds = jax.ShapeDtypeStruct
    out_shape = (
        sds((bsz, MLA_HEADS, s, LANES), BF16), sds((bsz, MLA_HEADS, s, LANES), BF16),
        sds((bsz, 2, ns, LANES, tm), BF16),
        sds((bsz, s, DIFF_QK_WIDTH), BF16), sds((bsz, s, DIFF_QK_WIDTH), BF16), sds((bsz, 2, ns, LANES, tm), BF16),
        sds((bsz, s, HGRN_WIDTH), F32), sds((bsz, s, HGRN_WIDTH), BF16),
        sds((bsz, s, 2 * HGRN_WIDTH), F32), sds((bsz, s, HGRN_WIDTH), F32),
    )
    out_specs = (head4, head4, vt, tok(DIFF_QK_WIDTH), tok(DIFF_QK_WIDTH), vt,
                 tok(HGRN_WIDTH), tok(HGRN_WIDTH), tok(2 * HGRN_WIDTH), tok(HGRN_WIDTH))
    return pl.pallas_call(
        _pre_kernel, grid=grid, in_specs=in_specs, out_specs=out_specs, out_shape=out_shape,
        compiler_params=_params(2), name='pre_mixer',
    )(xall, mods3, g1, win, gq, wuq, gkv, wukv, llb, l1mlb, *tables)


def _attend(streams, n_groups, tkg, dv, s_ref):
    tq = s_ref.shape[-1]

    ones_rows = (lax.broadcasted_iota(jnp.int32, (BF16_SUBLANES, tkg), 0) == 0).astype(BF16)

    def score(si, g, slot):
        q, key_rows, _ = streams[si]
        s = _dot_nt(key_rows(g), q)
        s_ref[si, slot, 0:tkg, :] = s
        return jnp.max(s, axis=0, keepdims=True)

    def softmax_pv(si, g, slot, state, mx):
        m, acc = state
        m_new = jnp.maximum(m, mx)
        alpha = jnp.exp2(m - m_new)
        p = jnp.exp2(s_ref[si, slot, 0:tkg, :] - m_new).astype(BF16)
        v1 = jnp.concatenate([streams[si][2](g), ones_rows], axis=0)
        return m_new, alpha * acc + _dot(v1, p)

    n = len(streams)
    carry = [(jnp.full((1, tq), -jnp.inf, F32), jnp.zeros((dv + BF16_SUBLANES, tq), F32)) for _ in streams]
    maxes = [score(si, 0, 0) for si in range(n)]
    for g in range(n_groups):
        slot, next_slot = g % 2, (g + 1) % 2
        has_next = g + 1 < n_groups
        next_maxes = [None] * n
        if has_next:
            next_maxes[0] = score(0, g + 1, next_slot)
        for si in range(n):
            if has_next and si + 1 < n:
                next_maxes[si + 1] = score(si + 1, g + 1, next_slot)
            carry[si] = softmax_pv(si, g, slot, carry[si], maxes[si])
        maxes = next_maxes
    return [(acc[0:dv], acc[dv:dv + 1]) for (_, acc) in carry]


def _chunks_per_group(n_chunks):
    return next(u for u in (3, 4, 2, 1) if n_chunks % u == 0)


def _for_context_and_latent(i, ctx_chunks, all_chunks, run):
    @pl.when(i < ctx_chunks)
    def _():
        u = _chunks_per_group(ctx_chunks)
        run(ctx_chunks // u, u)

    @pl.when(i >= ctx_chunks)
    def _():
        u = _chunks_per_group(all_chunks)
        run(all_chunks // u, u)


def _key_rows(k_ref, lead, tk, u):
    return lambda g: k_ref[lead + (pl.ds(pl.multiple_of(g * (u * tk), u * tk), u * tk), slice(None))]


def _value_t(vt_ref, u, hh, dv, pair=0):
    def get(g):
        parts = [vt_ref[0, pair, g * u + c, hh * dv:(hh + 1) * dv, :] for c in range(u)]
        return parts[0] if u == 1 else jnp.concatenate(parts, axis=1)
    return get


def _score_scratch(n_streams, ctx_chunks, all_chunks, tk, tq):
    rows = max(_chunks_per_group(ctx_chunks), _chunks_per_group(all_chunks)) * tk
    return pltpu.VMEM((n_streams, 2, rows, tq), F32)


def _mla_kernel(ctx_chunks, q_ref, k_ref, vt_ref, o_ref, s_ref):
    tq = q_ref.shape[2]
    tk = vt_ref.shape[-1]

    def run(n_groups, u):
        streams = [(q_ref[0, h], _key_rows(k_ref, (0, h), tk, u), _value_t(vt_ref, u, h % 2, MLA_V, h // 2))
                   for h in range(MLA_HEADS)]
        res = _attend(streams, n_groups, u * tk, MLA_V, s_ref)
        ot = jnp.concatenate([acc / l for acc, l in res], axis=0)
        o_ref[0] = ot.T.astype(o_ref.dtype)

    _for_context_and_latent(pl.program_id(1), ctx_chunks, vt_ref.shape[2], run)


def _mla_attention(qm, km, vtm, ctx_chunks):
    bsz, _, s, _ = qm.shape
    nc, tk = vtm.shape[2], vtm.shape[4]
    tq = TOKEN_TILE
    return pl.pallas_call(
        functools.partial(_mla_kernel, ctx_chunks),
        grid=(bsz, s // tq),
        in_specs=[pl.BlockSpec((1, MLA_HEADS, tq, LANES), lambda b, i: (b, 0, i, 0)),
                  pl.BlockSpec((1, MLA_HEADS, s, LANES), lambda b, i: (b, 0, 0, 0)),
                  pl.BlockSpec((1, 2, nc, LANES, tk), lambda b, i: (b, 0, 0, 0, 0))],
        out_specs=pl.BlockSpec((1, tq, MLA_HEADS * MLA_V), lambda b, i: (b, i, 0)),
        out_shape=jax.ShapeDtypeStruct((bsz, s, MLA_HEADS * MLA_V), BF16),
        scratch_shapes=[_score_scratch(MLA_HEADS, ctx_chunks, nc, tk, tq)],
        compiler_params=_params(2), name='mla_attention',
    )(qm, km, vtm)


def _diff_kernel(ctx_chunks, lam_init, q_ref, k_ref, vt_ref, lam_ref, g_ref, o_ref, s_ref):
    tq = q_ref.shape[1]
    tk = vt_ref.shape[-1]

    def run(n_groups, u):
        lp = lam_ref[...]
        lam = (jnp.exp(jnp.sum(lp[0:1] * lp[1:2], axis=1, keepdims=True))
               - jnp.exp(jnp.sum(lp[2:3] * lp[3:4], axis=1, keepdims=True)) + lam_init)
        q = q_ref[0]
        lane = lax.broadcasted_iota(jnp.int32, q.shape, 1)
        key_rows = _key_rows(k_ref, (0,), tk, u)
        streams = []
        for h in range(DIFF_HEADS):
            for mp in range(2):
                base = (h * 2 + mp) * DIFF_DK
                qsel = jnp.where((lane >= base) & (lane < base + DIFF_DK), q, jnp.zeros_like(q))
                streams.append((qsel, key_rows, _value_t(vt_ref, u, h % 2, DIFF_DV, h // 2)))
        res = _attend(streams, n_groups, u * tk, DIFF_DV, s_ref)
        outs = []
        for h in range(DIFF_HEADS):
            (a1, l1), (a2, l2) = res[2 * h], res[2 * h + 1]
            outs.append(a1 / l1 - lam * (a2 / l2))
        normed = [o * lax.rsqrt(jnp.mean(o * o, axis=0, keepdims=True) + EPS) for o in outs]
        y = jnp.concatenate(normed, axis=0) * g_ref[...] * (1 - lam_init)
        o_ref[0] = y.T.astype(o_ref.dtype)

    _for_context_and_latent(pl.program_id(1), ctx_chunks, vt_ref.shape[2], run)


def _diff_attention(dq, dk, vtd, lam_params, g_col, ctx_chunks, lam_init):
    bsz, s, w = dq.shape
    nc, tk = vtd.shape[2], vtd.shape[4]
    tq = TOKEN_TILE
    return pl.pallas_call(
        functools.partial(_diff_kernel, ctx_chunks, lam_init),
        grid=(bsz, s // tq),
        in_specs=[pl.BlockSpec((1, tq, w), lambda b, i: (b, i, 0)),
                  pl.BlockSpec((1, s, w), lambda b, i: (b, 0, 0)),
                  pl.BlockSpec((1, 2, nc, LANES, tk), lambda b, i: (b, 0, 0, 0, 0)),
                  pl.BlockSpec(lam_params.shape, lambda b, i: (0, 0)),
                  pl.BlockSpec(g_col.shape, lambda b, i: (0, 0))],
        out_specs=pl.BlockSpec((1, tq, DIFF_HEADS * DIFF_DV), lambda b, i: (b, i, 0)),
        out_shape=jax.ShapeDtypeStruct((bsz, s, DIFF_HEADS * DIFF_DV), BF16),
        scratch_shapes=[_score_scratch(2 * DIFF_HEADS, ctx_chunks, nc, tk, tq)],
        compiler_params=_params(2), name='diff_attention',
    )(dq, dk, vtd, lam_params, g_col)


def _hgrn_kernel(reverse, q_ref, v_ref, lf_ref, o_ref, st_ref):
    c = SCAN_CHUNK
    tile, width = q_ref.shape[1], q_ref.shape[2]
    n_ch, pairs = tile // c, width // LANES
    log_c = c.bit_length() - 1
    tok = lax.broadcasted_iota(jnp.int32, (c, 2 * c), 0)
    src = lax.broadcasted_iota(jnp.int32, (c, 2 * c), 1) & (c - 1)
    pos_t = (c - 1 - tok) if reverse else tok
    pos_s = (c - 1 - src) if reverse else src
    tok_col = lax.broadcasted_iota(jnp.int32, (tile, 1), 0) & (c - 1)
    pos_col = (c - 1 - tok_col) if reverse else tok_col
    head_a_all = (lax.broadcasted_iota(jnp.int32, (tile, width), 1) & (LANES - 1)) < HGRN_DK
    r2 = lax.broadcasted_iota(jnp.int32, (LANES, LANES), 0)
    c2 = lax.broadcasted_iota(jnp.int32, (LANES, LANES), 1)
    same_head = (r2 < HGRN_DV) == (c2 < HGRN_DK)
    tok2 = lax.broadcasted_iota(jnp.int32, (tile, tile), 0)
    src2 = lax.broadcasted_iota(jnp.int32, (tile, tile), 1)
    tri = (((tok2 >> log_c) == (src2 >> log_c)) & ((src2 >= tok2) if reverse else (src2 <= tok2))).astype(BF16)
    earlier = lambda x, n: pltpu.roll(x, (tile - n) if reverse else n, 0)
    later_by = lambda x, n: pltpu.roll(x, n if reverse else (tile - n), 0)

    @pl.when(pl.program_id(1) == 0)
    def _():
        st_ref[...] = jnp.zeros_like(st_ref)

    rows = lambda ci: slice(ci * c, (ci + 1) * c)
    lanes = lambda p: slice(p * LANES, (p + 1) * LANES)
    cells = [(ci, p) for ci in range(n_ch) for p in range(pairs)]

    q, v, lf = q_ref[0], v_ref[0], lf_ref[0]
    p0 = lf.astype(BF16)
    r0 = lf - p0.astype(F32)
    p1 = r0.astype(BF16)
    p2 = (r0 - p1.astype(F32)).astype(BF16)
    cum = (_dot(tri, p0) + _dot(tri, p1) + _dot(tri, p2)) * LOG2_E
    k = 1.0 - jnp.exp(lf)
    zero = jnp.zeros((), BF16)
    q16 = q.astype(BF16)
    k16 = k.astype(BF16)
    ka, kb = jnp.where(head_a_all, k16, zero), jnp.where(head_a_all, zero, k16)
    va, vb = jnp.where(head_a_all, v, zero), jnp.where(head_a_all, zero, v)

    scores = {cell: None for cell in cells}

    def add_stage(qd, kda, kdb, keep):
        for ci, p in cells:
            kd2 = jnp.concatenate([kda[rows(ci), lanes(p)], kdb[rows(ci), lanes(p)]], axis=0)
            sc = jnp.where(keep, _dot_nt(qd[rows(ci), lanes(p)], kd2), 0.0)
            scores[ci, p] = sc if scores[ci, p] is None else scores[ci, p] + sc

    end = cum
    block = 2
    while block <= c:
        half = block // 2
        first = (pos_col & (block - 1)) < half
        ref = jnp.where(first, end, earlier(end, half))
        end = jnp.where(first, later_by(end, half), end)
        e = jnp.exp2(-jnp.abs(cum - ref)).astype(BF16)
        e_q = jnp.where(first, zero, e)
        e_k = jnp.where(first, e, zero)
        shift = block.bit_length() - 1
        add_stage(q16 * e_q, ka * e_k, kb * e_k, (pos_t >> shift) == (pos_s >> shift))
        block *= 2
    add_stage(q16, ka, kb, tok == src)
    tot = end
    qs = (q * jnp.exp2(cum)).astype(BF16)
    ks = (k * jnp.exp2(tot - cum)).astype(BF16)

    intra, upd = {}, {}
    for ci, p in cells:
        v2 = jnp.concatenate([va[rows(ci), lanes(p)], vb[rows(ci), lanes(p)]], axis=0)
        intra[ci, p] = _dot(scores[ci, p].astype(BF16), v2)
        vt = v[rows(ci), lanes(p)].astype(F32).T.astype(BF16)
        upd[ci, p] = jnp.where(same_head, _dot(vt, ks[rows(ci), lanes(p)]), 0.0)

    states = [st_ref[p] for p in range(pairs)]
    for ci in (range(n_ch - 1, -1, -1) if reverse else range(n_ch)):
        decay = jnp.exp2(tot[ci * c:ci * c + 1, :])
        outs = []
        for p in range(pairs):
            outs.append(_dot_nt(qs[rows(ci), lanes(p)], states[p].astype(BF16)) + intra[ci, p])
            states[p] = states[p] * decay[:, lanes(p)] + upd[ci, p]
        o_ref[0, rows(ci), :] = jnp.concatenate(outs, axis=1)
    for p in range(pairs):
        st_ref[p] = states[p]


def _hgrn_scan(hq, hv, lf, reverse, ctx_tiles):
    bsz, s, w = hq.shape
    tile = TOKEN_TILE
    ns = s // tile
    if reverse:
        blk = lambda i: jnp.where(i < ctx_tiles, ctx_tiles - 1 - i, ns - 1 + ctx_tiles - i)
    else:
        blk = lambda i: i
    spec = lambda lane_block: pl.BlockSpec((1, tile, w), lambda b, i: (b, blk(i), lane_block))
    return pl.pallas_call(
        functools.partial(_hgrn_kernel, reverse),
        grid=(bsz, ns),
        in_specs=[spec(0), spec(0), spec(1 if reverse else 0)],
        out_specs=spec(0),
        out_shape=jax.ShapeDtypeStruct((bsz, s, w), F32),
        scratch_shapes=[pltpu.VMEM((w // LANES, LANES, LANES), F32)],
        compiler_params=_params(2), name='hgrn_bwd' if reverse else 'hgrn_fwd',
    )(hq, hv, lf)


def _post_kernel(ffn_chunk, x_ref, mod_ref, mla_ref, diff_ref, of_ref, ob_ref, hg_ref, gh_ref, seg_ref,
                 wo_ref, g2_ref, wg_ref, wu_ref, wd_ref, o_ref):
    d = x_ref.shape[-1]
    x = x_ref[0]
    mod = mod_ref[0]
    gt1, sh2, sc2, gt2 = (mod[:, 2 * d:3 * d], mod[:, 3 * d:4 * d], mod[:, 4 * d:5 * d], mod[:, 5 * d:6 * d])
    o = of_ref[0] + ob_ref[0]
    sq = o * o
    sq_hi = sq.astype(BF16)
    sq_lo = (sq - sq_hi.astype(F32)).astype(BF16)
    seg = seg_ref[...]
    ms = (_dot(sq_hi, seg) + _dot(sq_lo, seg)) * (1.0 / HGRN_DV)
    hn = (o * lax.rsqrt(ms + EPS) * gh_ref[...] * _silu(hg_ref[0])).astype(BF16)
    na, nb = mla_ref.shape[-1], diff_ref.shape[-1]
    y = (_dot(mla_ref[0], wo_ref[0:na, :]) + _dot(diff_ref[0], wo_ref[na:na + nb, :])
         + _dot(hn, wo_ref[na + nb:, :]))
    x1 = x + gt1 * y
    h2 = (_rms(x1, g2_ref[...]) * (1 + sc2) + sh2).astype(BF16)
    dff = wg_ref.shape[1]
    acc = jnp.zeros_like(x1)
    for lo in range(0, dff, ffn_chunk):
        g = _dot(h2, wg_ref[:, lo:lo + ffn_chunk])
        u = _dot(h2, wu_ref[:, lo:lo + ffn_chunk])
        acc = acc + _dot((_silu(g) * u).astype(BF16), wd_ref[lo:lo + ffn_chunk, :])
    o_ref[0] = x1 + gt2 * acc


def _post_mixer(xall, mods3, ctx_tiles, layer, mla, diff, o_f, o_b, hg, gh, seg, wo, g2, wg, wu, wd):
    bsz, s, d = xall.shape
    tm = TOKEN_TILE
    ffn_chunk = wg.shape[-1] // 2
    tok = lambda w: pl.BlockSpec((1, tm, w), lambda b, i: (b, i, 0))
    in_specs = [
        tok(d),
        _mod_spec(mods3, ctx_tiles),
        tok(mla.shape[-1]), tok(diff.shape[-1]), tok(HGRN_WIDTH), tok(HGRN_WIDTH), tok(HGRN_WIDTH),
        _const_spec(gh.shape), _const_spec(seg.shape), _layer_spec(wo, layer), _const_spec(g2.shape),
        _layer_spec(wg, layer), _layer_spec(wu, layer), _layer_spec(wd, layer),
    ]
    return pl.pallas_call(
        functools.partial(_post_kernel, ffn_chunk),
        grid=(bsz, s // tm), in_specs=in_specs, out_specs=tok(d),
        out_shape=jax.ShapeDtypeStruct((bsz, s, d), F32),
        compiler_params=_params(2), name='post_mixer',
    )(xall, mods3, mla, diff, o_f, o_b, hg, gh, seg, wo, g2, wg, wu, wd)


def _final_kernel(x_ref, g_ref, o_ref):
    o_ref[0] = _rms(x_ref[0], g_ref[...])


def _final_norm(xall, g, ctx_tiles, seq):
    bsz, _, d = xall.shape
    tm = TOKEN_TILE
    return pl.pallas_call(
        _final_kernel,
        grid=(bsz, seq // tm),
        in_specs=[pl.BlockSpec((1, tm, d), lambda b, i: (b, i + ctx_tiles, 0)),
                  pl.BlockSpec(g.shape, lambda b, i: (0, 0))],
        out_specs=pl.BlockSpec((1, tm, d), lambda b, i: (b, i, 0)),
        out_shape=jax.ShapeDtypeStruct((bsz, seq, d), F32),
        compiler_params=_params(2), name='final_norm',
    )(xall, g)


def _split_cols(w, sizes):
    return jnp.split(w, np.cumsum(sizes)[:-1].tolist(), axis=-1)


def _swap32(w):
    lead = w.shape[:-1]
    return w.reshape(*lead, -1, ROPE_DIM)[..., _ROPE_SWAP].reshape(*lead, -1)


def _layout_weights(w_in, w_uq, w_ukv):
    depth, d, _ = w_in.shape
    in_sizes = (MLA_Q_RANK, MLA_KV_RANK, MLA_ROPE, DIFF_QK_WIDTH, DIFF_QK_WIDTH, DIFF_WIDTH,
                HGRN_WIDTH, HGRN_WIDTH, HGRN_WIDTH, HGRN_WIDTH, HGRN_WIDTH)
    cq, ckv, kr, dq, dk, dv, hq, hff, hfb, hi, hg = _split_cols(w_in, in_sizes)
    zeros = lambda rows, n: jnp.zeros((depth, rows, n), F32)
    rope_tile = lambda w: jnp.concatenate([zeros(w.shape[1], MLA_NOPE), w, zeros(w.shape[1], LANES - MLA_NOPE - MLA_ROPE)], -1)
    dqs = dq * (DIFF_SCALE * LOG2_E)
    win = jnp.concatenate([cq, ckv, rope_tile(kr), rope_tile(_swap32(kr)), dqs, _swap32(dqs), dk, _swap32(dk), dv,
                           hq, hff, hfb, hi, hg], axis=-1).astype(BF16)
    per = MLA_NOPE + MLA_ROPE
    uq_tiles, uq_sw_tiles = [], []
    for h in range(MLA_HEADS):
        nope = w_uq[..., h * per:h * per + MLA_NOPE]
        rope = w_uq[..., h * per + MLA_NOPE:(h + 1) * per]
        uq_tiles.append(jnp.concatenate([nope, rope, zeros(MLA_Q_RANK, LANES - per)], -1))
        uq_sw_tiles.append(rope_tile(_swap32(rope)))
    wuq = (jnp.concatenate(uq_tiles + uq_sw_tiles, axis=-1) * (MLA_SCALE * LOG2_E)).astype(BF16)
    kv = MLA_NOPE + MLA_V
    k_tiles = [jnp.concatenate([w_ukv[..., h * kv:h * kv + MLA_NOPE], zeros(MLA_KV_RANK, LANES - MLA_NOPE)], -1)
               for h in range(MLA_HEADS)]
    v_cols = [w_ukv[..., h * kv + MLA_NOPE:(h + 1) * kv] for h in range(MLA_HEADS)]
    wukv = jnp.concatenate(k_tiles + v_cols, axis=-1).astype(BF16)
    return win, wuq, wukv


def _rope_tables(seq, ctx_len):
    t = jnp.arange(seq)
    row = (t // GRID_W).astype(F32)
    col = (t % GRID_W).astype(F32)
    n_freq = ROPE_DIM // 4
    freqs = ROPE_BASE ** (-jnp.arange(n_freq, dtype=F32) / n_freq)
    ang_r = row[:, None] * freqs
    ang_c = col[:, None] * freqs
    cr, sr, cc, sc = jnp.cos(ang_r), jnp.sin(ang_r), jnp.cos(ang_c), jnp.sin(ang_c)
    c32 = jnp.concatenate([cr, cr, cc, cc], axis=-1)
    s32 = jnp.concatenate([-sr, sr, -sc, sc], axis=-1)
    c32 = jnp.concatenate([jnp.ones((ctx_len, ROPE_DIM), F32), c32], axis=0)
    s32 = jnp.concatenate([jnp.zeros((ctx_len, ROPE_DIM), F32), s32], axis=0)
    n = c32.shape[0]
    cd = jnp.tile(c32, (1, LANES // ROPE_DIM))
    sd = jnp.tile(s32, (1, LANES // ROPE_DIM))
    pad = jnp.zeros((n, LANES - MLA_NOPE - MLA_ROPE), F32)
    cq = jnp.concatenate([jnp.ones((n, MLA_NOPE), F32), c32, pad], axis=-1)
    sq = jnp.concatenate([jnp.zeros((n, MLA_NOPE), F32), s32, pad], axis=-1)
    return cd, sd, cq, sq


def kernel(x, c, ctx, c_ctx, w_ada, b_ada, g_norm1, g_norm2, w_in, g_q_norm, w_uq, g_kv_norm, w_ukv,
           diff_lambda, g_diff_norm, hgrn_lower_bounds, g_hgrn_norm, w_out, w_ffn_gate, w_ffn_up,
           w_ffn_down, g_final):
    bsz, seq, d = x.shape
    ctx_len = ctx.shape[1]
    depth = w_in.shape[0]
    assert ctx_len % TOKEN_TILE == 0 and seq % TOKEN_TILE == 0 and seq % GRID_W == 0
    ctx_tiles = ctx_len // TOKEN_TILE

    cvec = jnp.concatenate([c, c_ctx[None, :], jnp.zeros((SUBLANES - bsz - 1, d), F32)], axis=0)
    mods = _adaln(cvec.T, bsz + 1, w_ada, b_ada)
    win, wuq, wukv = _layout_weights(w_in, w_uq, w_ukv)
    wo, wg, wu, wd = (w.astype(BF16) for w in (w_out, w_ffn_gate, w_ffn_up, w_ffn_down))
    tables = _rope_tables(seq, ctx_len)

    p = jax.nn.softmax(hgrn_lower_bounds.astype(F32), axis=0)
    cum = jnp.cumsum(p, axis=0)
    lb = (cum - cum[0:1]).reshape(depth, 1, 2 * HGRN_WIDTH)
    llb, l1mlb = jnp.log(lb), jnp.log1p(-lb)

    idx = np.arange(HGRN_WIDTH)
    seg = jnp.asarray(idx[:, None] // HGRN_DV == idx[None, :] // HGRN_DV, BF16)

    xall = jnp.concatenate([ctx, x], axis=1)
    for l in range(depth):
        lam_init = 0.8 - 0.6 * math.exp(-0.3 * l)
        mods3 = mods[l, 0:bsz + 1].reshape(bsz + 1, 1, 6 * d)
        qm, km, vtm, dq, dk, vtd, hq, hv, lf, hg = _pre_mixer(
            xall, mods3, ctx_tiles, l, g_norm1[l][None], win, g_q_norm[l][None], wuq, g_kv_norm[l][None], wukv,
            llb[l], l1mlb[l], tables)
        mla = _mla_attention(qm, km, vtm, ctx_tiles)
        g_col = jnp.tile(g_diff_norm[l], DIFF_HEADS)[:, None]
        diff = _diff_attention(dq, dk, vtd, diff_lambda[l].astype(F32), g_col, ctx_tiles, lam_init)
        o_f = _hgrn_scan(hq, hv, lf, False, ctx_tiles)
        o_b = _hgrn_scan(hq, hv, lf, True, ctx_tiles)
        gh = jnp.tile(g_hgrn_norm[l], HGRN_HEADS)[None]
        xall = _post_mixer(xall, mods3, ctx_tiles, l, mla, diff, o_f, o_b, hg, gh, seg, wo, g_norm2[l][None],
                           wg, wu, wd)
    return _final_norm(xall, g_final[None], ctx_tiles, seq)
```
